```python
import jax, jax.numpy as jnp
from jax import lax
import numpy as np

D_MODEL = 1024
BATCH = 4
SEQ = 4096
DEPTH = 4
DEC_BATCH = 128
DEC_SEQ = 1
PAST_LEN = 8192
PAGE_SIZE = 128

N_EVEN = (DEPTH + 1) // 2
N_ODD = DEPTH // 2
A_WIDTH = D_MODEL // 2
A_CONV = 3
B_HEADS = 8
B_NOPE = 64
B_ROPE = 32
B_V = 64
B_Q_RANK = 384
B_KV_RANK = 256
ROPE_THETA = 10000.0
C_WIDTH = D_MODEL // 2
C_CONV = 31
D_HEADS = 8
D_HEAD_DIM = 64
D_WIDTH = D_HEADS * D_HEAD_DIM
F_BIAS_INIT = 3.0
D_FF = 4 * D_MODEL
Q_BLOCK = 128
ALPHA = (2 * DEPTH) ** 0.25
BETA = (8 * DEPTH) ** -0.25
MLA_SCALE = (B_NOPE + B_ROPE) ** -0.5
FOX_SCALE = D_HEAD_DIM ** -0.5
LN_EPS = 1e-5
RMS_EPS = 1e-6
EVEN_SPLITS = (A_WIDTH, A_WIDTH, A_WIDTH, B_Q_RANK, B_KV_RANK, B_ROPE)
ODD_SPLITS = (2 * C_WIDTH, D_WIDTH, D_WIDTH, D_WIDTH, D_HEADS)
EVEN_IN = sum(EVEN_SPLITS)
ODD_IN = sum(ODD_SPLITS)

kernel_name = 'hybrid_conv_mla_conformer_fox_decoder_step'


def split_cols(z, sizes):
    cuts = [int(c) for c in np.cumsum(sizes)[:-1]]
    return jnp.split(z, cuts, axis=-1)


def layer_norm(x, g, b):
    xf = x.astype(jnp.float32)
    mu = jnp.mean(xf, -1, keepdims=True)
    var = jnp.mean(jnp.square(xf - mu), -1, keepdims=True)
    return ((xf - mu) * lax.rsqrt(var + LN_EPS)).astype(x.dtype) * g + b


def rms_norm(x, g):
    xf = x.astype(jnp.float32)
    return (xf * lax.rsqrt(jnp.mean(xf * xf, -1, keepdims=True) + RMS_EPS)).astype(x.dtype) * g


def rope(x, pos):
    half = x.shape[-1] // 2
    inv = ROPE_THETA ** (-jnp.arange(half, dtype=jnp.float32) / half)
    ang = pos.astype(jnp.float32)[:, None] * inv
    ang = ang.reshape(ang.shape[:1] + (1,) * (x.ndim - 3) + ang.shape[1:])
    cos, sin = jnp.cos(ang).astype(x.dtype), jnp.sin(ang).astype(x.dtype)
    x1, x2 = x[..., :half], x[..., half:]
    return jnp.concatenate([x1 * cos - x2 * sin, x1 * sin + x2 * cos], axis=-1)


def causal_depthwise_conv(u_ext, w):
    return lax.conv_general_dilated(u_ext, w[:, None, :].astype(u_ext.dtype), window_strides=(1,),
                                    padding='VALID', dimension_numbers=('NWC', 'WIO', 'NWC'),
                                    feature_group_count=w.shape[1])


def sweep_query_blocks(fn, qs, q_pos):
    t = q_pos.shape[0]
    if t <= Q_BLOCK or t % Q_BLOCK:
        return fn(qs, q_pos)
    nb = t // Q_BLOCK
    to_blocks = lambda a: jnp.moveaxis(a.reshape((a.shape[0], nb, Q_BLOCK) + a.shape[2:]), 1, 0)
    out = lax.map(lambda a: fn(a[0], a[1]), (tuple(to_blocks(a) for a in qs), q_pos.reshape(nb, Q_BLOCK)))
    out = jnp.moveaxis(out, 0, 1)
    return out.reshape((out.shape[0], t) + out.shape[3:])


def attend(s, v, spec):
    return jnp.einsum(spec, jax.nn.softmax(s, axis=-1).astype(v.dtype), v)


def paged_attend(page_scores, page_values, page_table, page_args, s_new, v_new, spec):
    n_pages = page_table.shape[1]
    pt = page_table.T
    s_past = lax.map(lambda a: page_scores(*a), (pt,) + page_args)
    b, h, q = s_past.shape[1:4]
    n_past = n_pages * PAGE_SIZE
    s_past = jnp.moveaxis(s_past, 0, 3).reshape(b, h, q, n_past)
    p = jax.nn.softmax(jnp.concatenate([s_past, s_new], axis=-1), axis=-1)
    p_past = jnp.moveaxis(p[..., :n_past].reshape(b, h, q, n_pages, PAGE_SIZE), 3, 0)
    o_past = lax.map(lambda a: jnp.einsum(spec, a[1].astype(v_new.dtype), page_values(a[0])), (pt, p_past))
    o_past = jnp.sum(o_past.astype(jnp.float32), axis=0).astype(v_new.dtype)
    return o_past + jnp.einsum(spec, p[..., n_past:].astype(v_new.dtype), v_new)


def mla_scores(q_lat, q_rope, lat, k_rope):
    s = jnp.einsum('bqhr,bkr->bhqk', q_lat, lat) + jnp.einsum('bqhp,bkp->bhqk', q_rope, k_rope)
    return s.astype(jnp.float32) * MLA_SCALE


def fox_scores(q, fq, k, fk):
    s = jnp.einsum('bqhd,bkhd->bhqk', q, k).astype(jnp.float32) * FOX_SCALE
    return s + jnp.swapaxes(fq, 1, 2)[..., :, None] - jnp.swapaxes(fk, 1, 2)[..., None, :]


def even_mixer(h, pos, w_in, w_q_b, w_kv_b, q_norm, kv_norm, conv_a, w_out, past):
    n, t, _ = h.shape
    xa, gb, gc, q_a, kv_a, kr = split_cols(h @ w_in, EVEN_SPLITS)
    u = gc * xa
    hist = jnp.zeros((n, A_CONV - 1, A_WIDTH), h.dtype) if past is None else past[0].astype(h.dtype)
    u_ext = jnp.concatenate([hist, u], axis=1)
    y_a = gb * causal_depthwise_conv(u_ext, conv_a)
    q = (rms_norm(q_a, q_norm) @ w_q_b).reshape(n, t, B_HEADS, B_NOPE + B_ROPE)
    q_rope = rope(q[..., B_NOPE:], pos)
    q_lat = jnp.einsum('bqhn,rhn->bqhr', q[..., :B_NOPE], w_kv_b[..., :B_NOPE])
    c_kv = rms_norm(kv_a, kv_norm)
    k_rope = rope(kr, pos)
    spec = 'bhqk,bkr->bqhr'

    def block(qs, qp):
        ql, qr = qs
        s_new = jnp.where(qp[:, None] >= pos[None, :], mla_scores(ql, qr, c_kv, k_rope), -jnp.inf)
        if past is None:
            return attend(s_new, c_kv, spec)
        _, lat_pool, kr_pool, layer, page_table = past
        return paged_attend(lambda pages: mla_scores(ql, qr, lat_pool[layer, pages], kr_pool[layer, pages]),
                            lambda pages: lat_pool[layer, pages], page_table, (), s_new, c_kv, spec)

    o_lat = sweep_query_blocks(block, (q_lat, q_rope), pos)
    y_b = jnp.einsum('bqhr,rhv->bqhv', o_lat, w_kv_b[..., B_NOPE:]).reshape(n, t, B_HEADS * B_V)
    out = jnp.concatenate([y_a, y_b], axis=-1) @ w_out
    return out, (c_kv, k_rope, u_ext[:, -(A_CONV - 1):])


def odd_mixer(h, pos, w_in, b_f, conv_c, conv_c_b, cn_g, cn_b, w_out, past):
    n, t, _ = h.shape
    glu, q, k, v, zf = split_cols(h @ w_in, ODD_SPLITS)
    u = glu[..., :C_WIDTH] * jax.nn.sigmoid(glu[..., C_WIDTH:])
    hist = jnp.zeros((n, C_CONV - 1, C_WIDTH), h.dtype) if past is None else past[0].astype(h.dtype)
    u_ext = jnp.concatenate([hist, u], axis=1)
    y_c = jax.nn.silu(layer_norm(causal_depthwise_conv(u_ext, conv_c) + conv_c_b, cn_g, cn_b))
    q, k, v = (a.reshape(n, t, D_HEADS, D_HEAD_DIM) for a in (q, k, v))
    logf = jax.nn.log_sigmoid((zf + b_f).astype(jnp.float32))
    spec = 'bhqk,bkhd->bqhd'
    if past is None:
        f_new = jnp.cumsum(logf, axis=1)
        f_past_pages = None
    else:
        _, k_pool, v_pool, logf_pool, layer, page_table = past
        logf_past = logf_pool[layer, page_table].reshape(n, -1, D_HEADS).astype(jnp.float32)
        n_past = logf_past.shape[1]
        f_all = jnp.cumsum(jnp.concatenate([logf_past, logf], axis=1), axis=1)
        f_past_pages = jnp.moveaxis(f_all[:, :n_past].reshape(n, -1, PAGE_SIZE, D_HEADS), 1, 0)
        f_new = f_all[:, n_past:]

    def block(qs, qp):
        qb, fq = qs
        s_new = jnp.where(qp[:, None] >= pos[None, :], fox_scores(qb, fq, k, f_new), -jnp.inf)
        if past is None:
            return attend(s_new, v, spec)
        return paged_attend(lambda pages, fk: fox_scores(qb, fq, k_pool[layer, pages], fk),
                            lambda pages: v_pool[layer, pages], page_table, (f_past_pages,), s_new, v, spec)

    o = sweep_query_blocks(block, (q, f_new), pos)
    out = jnp.concatenate([y_c, o.reshape(n, t, D_WIDTH)], axis=-1) @ w_out
    return out, (k, v, logf.astype(h.dtype), u_ext[:, -(C_CONV - 1):])


def setup_inputs(seed: int = 0) -> dict:
    key = jax.random.key(seed)
    ks = iter(jax.random.split(key, 48))

    def nrm(shape, scale=1.0):
        return jax.random.normal(next(ks), shape, jnp.float32) * scale

    def gain(shape):
        return 1.0 + nrm(shape, 0.02)

    n_pages = PAST_LEN // PAGE_SIZE
    n_used = DEC_BATCH * n_pages
    n_pool = n_used + n_used // 4 + 1
    page_table = jax.random.permutation(next(ks), n_pool)[:n_used].reshape(DEC_BATCH, n_pages).astype(jnp.int32)
    return {
        'x_prompt': nrm((BATCH, SEQ, D_MODEL)),
        'x_sample': nrm((DEC_BATCH, DEC_SEQ, D_MODEL)),
        'cache_mla_latent': nrm((N_EVEN, n_pool, PAGE_SIZE, B_KV_RANK)),
        'cache_mla_krope': nrm((N_EVEN, n_pool, PAGE_SIZE, B_ROPE)),
        'state_conv_a': nrm((N_EVEN, DEC_BATCH, A_CONV - 1, A_WIDTH), 0.5),
        'cache_fox_k': nrm((N_ODD, n_pool, PAGE_SIZE, D_HEADS, D_HEAD_DIM)),
        'cache_fox_v': nrm((N_ODD, n_pool, PAGE_SIZE, D_HEADS, D_HEAD_DIM)),
        'cache_fox_logf': jax.nn.log_sigmoid(F_BIAS_INIT + nrm((N_ODD, n_pool, PAGE_SIZE, D_HEADS))),
        'state_conv_c': nrm((N_ODD, DEC_BATCH, C_CONV - 1, C_WIDTH), 0.5),
        'page_table': page_table,
        'c_prompt': nrm((BATCH, D_MODEL)),
        'c_sample': nrm((DEC_BATCH, D_MODEL)),
        'w_ada': nrm((DEPTH, D_MODEL, 6 * D_MODEL), 0.1 * D_MODEL ** -0.5),
        'b_ada': nrm((DEPTH, 6 * D_MODEL), 0.01),
        'ln_g': gain((DEPTH, 2, D_MODEL)),
        'ln_b': nrm((DEPTH, 2, D_MODEL), 0.01),
        'w_in_even': nrm((N_EVEN, D_MODEL, EVEN_IN), D_MODEL ** -0.5),
        'w_q_b': nrm((N_EVEN, B_Q_RANK, B_HEADS * (B_NOPE + B_ROPE)), B_Q_RANK ** -0.5),
        'w_kv_b': nrm((N_EVEN, B_KV_RANK, B_HEADS, B_NOPE + B_V), B_KV_RANK ** -0.5),
        'q_norm': gain((N_EVEN, B_Q_RANK)),
        'kv_norm': gain((N_EVEN, B_KV_RANK)),
        'conv_a': nrm((N_EVEN, A_CONV, A_WIDTH), A_CONV ** -0.5),
        'w_out_even': nrm((N_EVEN, A_WIDTH + B_HEADS * B_V, D_MODEL), BETA * (A_WIDTH + B_HEADS * B_V) ** -0.5),
        'w_in_odd': nrm((N_ODD, D_MODEL, ODD_IN), D_MODEL ** -0.5),
        'b_f': F_BIAS_INIT + nrm((N_ODD, D_HEADS), 0.1),
        'conv_c': nrm((N_ODD, C_CONV, C_WIDTH), C_CONV ** -0.5),
        'conv_c_b': nrm((N_ODD, C_WIDTH), 0.01),
        'cn_g': gain((N_ODD, C_WIDTH)),
        'cn_b': nrm((N_ODD, C_WIDTH), 0.01),
        'w_out_odd': nrm((N_ODD, C_WIDTH + D_WIDTH, D_MODEL), BETA * (C_WIDTH + D_WIDTH) ** -0.5),
        'w_ff1': nrm((DEPTH, D_MODEL, D_FF), D_MODEL ** -0.5),
        'w_ff2': nrm((DEPTH, D_FF, D_MODEL), BETA * D_FF ** -0.5),
    }


def reference(x_prompt, x_sample, cache_mla_latent, cache_mla_krope, state_conv_a, cache_fox_k, cache_fox_v,
              cache_fox_logf, state_conv_c, page_table, c_prompt, c_sample, w_ada, b_ada, ln_g, ln_b,
              w_in_even, w_q_b, w_kv_b, q_norm, kv_norm, conv_a, w_out_even, w_in_odd, b_f, conv_c,
              conv_c_b, cn_g, cn_b, w_out_odd, w_ff1, w_ff2):
    def trunk(x, c, pos, sample):
        even_states, odd_states = [], []
        for l in range(DEPTH):
            i = l // 2
            mod = (jax.nn.silu(c) @ w_ada[l] + b_ada[l])[:, None, :]
            sh1, sc1, g1, sh2, sc2, g2 = jnp.split(mod, 6, axis=-1)
            h = x * (1 + sc1) + sh1
            if l % 2 == 0:
                past = (state_conv_a[i], cache_mla_latent, cache_mla_krope, i, page_table) if sample else None
                mix, st = even_mixer(h, pos, w_in_even[i], w_q_b[i], w_kv_b[i], q_norm[i], kv_norm[i],
                                     conv_a[i], w_out_even[i], past)
                even_states.append(st)
            else:
                past = (state_conv_c[i], cache_fox_k, cache_fox_v, cache_fox_logf, i, page_table) if sample else None
                mix, st = odd_mixer(h, pos, w_in_odd[i], b_f[i], conv_c[i], conv_c_b[i], cn_g[i], cn_b[i],
                                    w_out_odd[i], past)
                odd_states.append(st)
            x = layer_norm(ALPHA * x + (1 + g1) * mix, ln_g[l, 0], ln_b[l, 0])
            h = x * (1 + sc2) + sh2
            ff = jnp.square(jax.nn.relu(h @ w_ff1[l])) @ w_ff2[l]
            x = layer_norm(ALPHA * x + (1 + g2) * ff, ln_g[l, 1], ln_b[l, 1])
        stack = lambda states: [jnp.stack(s) for s in zip(*states)]
        return x, stack(even_states), stack(odd_states)

    past_len = page_table.shape[1] * PAGE_SIZE
    y_p, (lat_p, kr_p, ca_p), (k_p, v_p, lf_p, cc_p) = trunk(
        x_prompt, c_prompt, jnp.arange(x_prompt.shape[1], dtype=jnp.int32), False)
    y_s, (lat_s, kr_s, ca_s), (k_s, v_s, lf_s, cc_s) = trunk(
        x_sample, c_sample, past_len + jnp.arange(x_sample.shape[1], dtype=jnp.int32), True)
    return (y_p, y_s, lat_p, kr_p, ca_p, k_p, v_p, lf_p, cc_p, lat_s, kr_s, ca_s, k_s, v_s, lf_s, cc_s)
```

```python
import functools

import numpy as np
import jax
import jax.numpy as jnp
from jax import lax
from jax.experimental import pallas as pl
from jax.experimental.pallas import tpu as pltpu

F32 = jnp.float32
BF16 = jnp.bfloat16

A_CONV = 3
C_CONV = 31
B_HEADS = 8
B_NOPE = 64
B_ROPE = 32
B_V = 64
D_HEADS = 8
D_HEAD_DIM = 64
ROPE_THETA = 10000.0
MLA_SCALE = (B_NOPE + B_ROPE) ** -0.5
FOX_SCALE = D_HEAD_DIM ** -0.5
LN_EPS = 1e-5
RMS_EPS = 1e-6
NEG = -1e30
LANE = 128
VMEM_LIMIT = 56 * 1024 * 1024

_NT = (((1,), (1,)), ((), ()))


def _cp(*sem):
    return pltpu.CompilerParams(dimension_semantics=sem, vmem_limit_bytes=VMEM_LIMIT)


def _ln(y, g, b):
    mu = jnp.mean(y, -1, keepdims=True)
    d = y - mu
    var = jnp.mean(d * d, -1, keepdims=True)
    return d * lax.rsqrt(var + LN_EPS) * g + b


def _rms(y, g):
    return y * lax.rsqrt(jnp.mean(y * y, -1, keepdims=True) + RMS_EPS) * g


def _split3(x):
    hi = x.astype(BF16)
    r = x - hi.astype(F32)
    mid = r.astype(BF16)
    lo = (r - mid.astype(F32)).astype(BF16)
    return hi, mid, lo


def _dot(a, b):
    return jnp.dot(a, b, preferred_element_type=F32)


def _dot_nt(a, b):
    return lax.dot_general(a, b, _NT, preferred_element_type=F32)


def _mod_spec(r, d):
    if r == 1:
        return pl.BlockSpec((1, 6, 1, d), lambda b, i, *_: (b, 0, 0, 0))
    return pl.BlockSpec((1, 6, r, d), lambda b, i, *_: (b, 0, i, 0))


def _row_spec(tm, c):
    return pl.BlockSpec((1, tm, c), lambda b, i, *_: (b, i, 0))


def _full_spec(shape):
    return pl.BlockSpec(shape, lambda *_: (0,) * len(shape))


def _ada_kernel(c_ref, w_ref, b_ref, o_ref):
    c = c_ref[...]
    s = (c * jax.nn.sigmoid(c)).astype(BF16)
    o_ref[0, 0] = _dot(s, w_ref[0].astype(BF16)) + b_ref[0]


def _ada(c_all, w_ada, b_ada):
    depth, d, d6 = w_ada.shape
    n = c_all.shape[0]
    return pl.pallas_call(
        _ada_kernel, grid=(depth, d6 // d),
        in_specs=[pl.BlockSpec((n, d), lambda l, j: (0, 0)),
                  pl.BlockSpec((1, d, d), lambda l, j: (l, 0, j)),
                  pl.BlockSpec((1, 1, d), lambda l, j: (l, 0, j))],
        out_specs=pl.BlockSpec((1, 1, n, d), lambda l, j: (l, j, 0, 0)),
        out_shape=jax.ShapeDtypeStruct((depth, d6 // d, n, d), F32),
        compiler_params=_cp("arbitrary", "arbitrary"), name="ada")(c_all, w_ada, b_ada.reshape(depth, 1, d6))


E_XA, E_GB, E_GC, E_QA, E_KV, E_KR, E_KRR, E_END = 0, 512, 1024, 1536, 1920, 2176, 2208, 2240
E_COLS = 2304
Q_NOPE, Q_ROPE, Q_ROT, Q_END = 0, 512, 768, 1024


def _even_in_kernel(x_ref, mod_ref, w_ref, qn_ref, kvn_ref, wq_ref, ck_ref, sk_ref, cq_ref, sq_ref,
                    u_ref, gb_ref, ckv_ref, ckvb_ref, kr_ref, krb_ref, qnope_ref, qrope_ref):
    m = mod_ref[0]
    h = x_ref[0] * (1.0 + m[1]) + m[0]
    z = _dot(h.astype(BF16), w_ref[...])
    u_ref[0] = z[:, E_GC:E_QA] * z[:, E_XA:E_GB]
    gb_ref[0] = z[:, E_GB:E_GC]
    ckv = _rms(z[:, E_KV:E_KR], kvn_ref[...])
    ckv_ref[0] = ckv
    ckvb_ref[0] = ckv.astype(BF16)
    kr = z[:, E_KR:E_KRR] * ck_ref[...] + z[:, E_KRR:E_END] * sk_ref[...]
    kr_ref[0] = kr
    krb_ref[0] = kr.astype(BF16)
    qn = _rms(z[:, E_QA:E_KV], qn_ref[...])
    q = _dot(qn.astype(BF16), wq_ref[...])
    qnope_ref[0] = q[:, Q_NOPE:Q_ROPE].astype(BF16)
    qr = q[:, Q_ROPE:Q_ROT] * cq_ref[...] + q[:, Q_ROT:Q_END] * sq_ref[...]
    qrope_ref[0] = (qr * MLA_SCALE).astype(BF16)


def _even_in(x, mod, w_in, qn, kvn, wq, ck, sk, cq, sq, tm):
    nb, t, d = x.shape
    r = mod.shape[2]
    aw = E_GB - E_XA
    rank = E_KR - E_KV
    outs = [(aw, F32), (aw, F32), (rank, F32), (rank, BF16), (B_ROPE, F32), (B_ROPE, BF16),
            (B_HEADS * B_NOPE, BF16), (B_HEADS * B_ROPE, BF16)]
    tab = lambda c: pl.BlockSpec((tm, c), lambda b, i: (i, 0))
    return pl.pallas_call(
        _even_in_kernel, grid=(nb, t // tm),
        in_specs=[_row_spec(tm, d), _mod_spec(r, d), _full_spec(w_in.shape), _full_spec(qn.shape),
                  _full_spec(kvn.shape), _full_spec(wq.shape), tab(B_ROPE), tab(B_ROPE),
                  tab(B_HEADS * B_ROPE), tab(B_HEADS * B_ROPE)],
        out_specs=[_row_spec(tm, c) for c, _ in outs],
        out_shape=[jax.ShapeDtypeStruct((nb, t, c), dt) for c, dt in outs],
        compiler_params=_cp("arbitrary", "arbitrary"), name="even_in")(x, mod, w_in, qn, kvn, wq, ck, sk, cq, sq)


def _causal_pairs(t, tq, tk):
    qi, kj, last = [], [], []
    for i in range(t // tq):
        n = ((i + 1) * tq - 1) // tk + 1
        for j in range(n):
            qi.append(i)
            kj.append(j)
            last.append(int(j == n - 1))
    return (np.asarray(qi, np.int32), np.asarray(kj, np.int32), np.asarray(last, np.int32))


def _mla_attn_kernel(qi_ref, kj_ref, last_ref, qn_ref, qr_ref, ckv_ref, kr_ref, wkn_ref, wkv_ref, yb_ref,
                     ql_s, qr_s, m_s, l_s, acc_s, *, tq, tk):
    step = pl.program_id(1)
    qi = qi_ref[step]
    kj = kj_ref[step]

    @pl.when(kj == 0)
    def _init():
        qn = qn_ref[0]
        qr = qr_ref[0]
        for h in range(B_HEADS):
            ql = _dot(qn[:, h * B_NOPE:(h + 1) * B_NOPE], wkn_ref[h]) * MLA_SCALE
            ql_s[h * tq:(h + 1) * tq, :] = ql.astype(BF16)
            qr_s[h * tq:(h + 1) * tq, :] = qr[:, h * B_ROPE:(h + 1) * B_ROPE]
        m_s[...] = jnp.full_like(m_s, NEG)
        l_s[...] = jnp.zeros_like(l_s)
        acc_s[...] = jnp.zeros_like(acc_s)

    def update(masked):
        ckv = ckv_ref[0]
        s = _dot_nt(ql_s[...], ckv) + _dot_nt(qr_s[...], kr_ref[0])
        if masked:
            rows = lax.broadcasted_iota(jnp.int32, s.shape, 0) & (tq - 1)
            cols = lax.broadcasted_iota(jnp.int32, s.shape, 1)
            s = jnp.where(rows - cols >= kj * tk - qi * tq, s, NEG)
        m_prev = m_s[...]
        m_new = jnp.maximum(m_prev, jnp.max(s, -1, keepdims=True))
        alpha = jnp.exp(m_prev - m_new)
        p = jnp.exp(s - m_new)
        l_s[...] = alpha * l_s[...] + jnp.sum(p, -1, keepdims=True)
        acc_s[...] = alpha * acc_s[...] + _dot(p.astype(BF16), ckv)
        m_s[...] = m_new

    needs_mask = (kj + 1) * tk - 1 > qi * tq

    @pl.when(needs_mask)
    def _masked():
        update(True)

    @pl.when(jnp.logical_not(needs_mask))
    def _plain():
        update(False)

    @pl.when(last_ref[step] == 1)
    def _fin():
        o = acc_s[...] / l_s[...]
        ys = [_dot(o[h * tq:(h + 1) * tq].astype(BF16), wkv_ref[h]) for h in range(B_HEADS)]
        yb_ref[0] = jnp.concatenate(ys, axis=1).astype(yb_ref.dtype)


def _mla_attn(qnope, qrope, ckvb, krb, wkn, wkv, tq, tk):
    nb, t, _ = qnope.shape
    rank = ckvb.shape[-1]
    qi, kj, last = _causal_pairs(t, tq, tk)
    grid_spec = pltpu.PrefetchScalarGridSpec(
        num_scalar_prefetch=3, grid=(nb, len(qi)),
        in_specs=[pl.BlockSpec((1, tq, qnope.shape[-1]), lambda b, s, qi, kj, la: (b, qi[s], 0)),
                  pl.BlockSpec((1, tq, qrope.shape[-1]), lambda b, s, qi, kj, la: (b, qi[s], 0)),
                  pl.BlockSpec((1, tk, rank), lambda b, s, qi, kj, la: (b, kj[s], 0)),
                  pl.BlockSpec((1, tk, B_ROPE), lambda b, s, qi, kj, la: (b, kj[s], 0)),
                  _full_spec(wkn.shape), _full_spec(wkv.shape)],
        out_specs=pl.BlockSpec((1, tq, B_HEADS * B_V), lambda b, s, qi, kj, la: (b, qi[s], 0)),
        scratch_shapes=[pltpu.VMEM((B_HEADS * tq, rank), BF16), pltpu.VMEM((B_HEADS * tq, B_ROPE), BF16),
                        pltpu.VMEM((B_HEADS * tq, 1), F32), pltpu.VMEM((B_HEADS * tq, 1), F32),
                        pltpu.VMEM((B_HEADS * tq, rank), F32)])
    return pl.pallas_call(
        functools.partial(_mla_attn_kernel, tq=tq, tk=tk), grid_spec=grid_spec,
        out_shape=jax.ShapeDtypeStruct((nb, t, B_HEADS * B_V), BF16),
        compiler_params=_cp("arbitrary", "arbitrary"), name="mla_attn")(
            jnp.asarray(qi), jnp.asarray(kj), jnp.asarray(last), qnope, qrope, ckvb, krb, wkn, wkv)


def _conv_prompt(u, halo_ref, ext_s, w_ref, i, k):
    tm = u.shape[0]
    hr = halo_ref.shape[1]
    ext_s[0:hr] = jnp.where(i > 0, halo_ref[0], 0.0)
    ext_s[hr:hr + tm] = u
    off = hr - (k - 1)
    acc = w_ref[k - 1:k] * u
    for j in range(k - 1):
        acc = acc + w_ref[j:j + 1] * ext_s[off + j:off + j + tm]
    return acc


def _conv_sample(u, hist_ref, w_ref, k):
    acc = w_ref[k - 1:k] * u
    for j in range(k - 1):
        acc = acc + w_ref[j:j + 1] * hist_ref[j]
    return acc


def _halo_spec(tm, hr, c):
    return pl.BlockSpec((1, hr, c), lambda b, i: (b, jnp.maximum(i * (tm // hr) - 1, 0), 0))


def _even_out_kernel(*refs, alpha, sample):
    if sample:
        (u_ref, hist_ref, gb_ref, yb_ref, x_ref, mod_ref, cw_ref, wa_ref, wb_ref, g_ref, b_ref, o_ref) = refs
        conv = _conv_sample(u_ref[0], hist_ref, cw_ref, A_CONV)
    else:
        (u_ref, halo_ref, gb_ref, yb_ref, x_ref, mod_ref, cw_ref, wa_ref, wb_ref, g_ref, b_ref, o_ref, ext_s) = refs
        conv = _conv_prompt(u_ref[0], halo_ref, ext_s, cw_ref, pl.program_id(1), A_CONV)
    ya = gb_ref[0] * conv
    mix = _dot(ya.astype(BF16), wa_ref[...]) + _dot(yb_ref[0], wb_ref[...])
    m = mod_ref[0]
    o_ref[0] = _ln(alpha * x_ref[0] + (1.0 + m[2]) * mix, g_ref[...], b_ref[...])


def _mixer_out(kernel, u, side, others, x, mod, tail, tm, alpha, sample, k):
    nb, t, d = x.shape
    c = u.shape[-1]
    r = mod.shape[2]
    hr = 8 if k - 1 <= 8 else 32
    if sample:
        side_arg, side_spec, scratch = side, _full_spec(side.shape), []
    else:
        side_arg, side_spec, scratch = u, _halo_spec(tm, hr, c), [pltpu.VMEM((tm + hr, c), F32)]
    return pl.pallas_call(
        functools.partial(kernel, alpha=alpha, sample=sample), grid=(nb, t // tm),
        in_specs=[_row_spec(tm, c), side_spec] + [_row_spec(tm, o.shape[-1]) for o in others]
        + [_row_spec(tm, d), _mod_spec(r, d)] + [_full_spec(w.shape) for w in tail],
        out_specs=_row_spec(tm, d), out_shape=jax.ShapeDtypeStruct((nb, t, d), F32),
        scratch_shapes=scratch, compiler_params=_cp("arbitrary", "arbitrary"),
        name=kernel.__name__.strip("_"))(u, side_arg, *others, x, mod, *tail)


def _ffn_kernel(x_ref, mod_ref, w1_ref, w2_ref, g_ref, b_ref, o_ref, h_s, acc_s, *, alpha):
    f = pl.program_id(2)

    @pl.when(f == 0)
    def _first():
        m = mod_ref[0]
        h_s[...] = (x_ref[0] * (1.0 + m[4]) + m[3]).astype(BF16)
        acc_s[...] = jnp.zeros_like(acc_s)

    a = jnp.square(jnp.maximum(_dot(h_s[...], w1_ref[...]), 0.0))
    acc_s[...] += _dot(a.astype(BF16), w2_ref[...])

    @pl.when(f == pl.num_programs(2) - 1)
    def _last():
        m = mod_ref[0]
        o_ref[0] = _ln(alpha * x_ref[0] + (1.0 + m[5]) * acc_s[...], g_ref[...], b_ref[...])


def _ffn(x, mod, w1, w2, g, b, tm, tf, alpha):
    nb, t, d = x.shape
    r = mod.shape[2]
    ff = w1.shape[1]
    return pl.pallas_call(
        functools.partial(_ffn_kernel, alpha=alpha), grid=(nb, t // tm, ff // tf),
        in_specs=[_row_spec(tm, d), _mod_spec(r, d),
                  pl.BlockSpec((d, tf), lambda b, i, f: (0, f)), pl.BlockSpec((tf, d), lambda b, i, f: (f, 0)),
                  _full_spec(g.shape), _full_spec(b.shape)],
        out_specs=_row_spec(tm, d), out_shape=jax.ShapeDtypeStruct((nb, t, d), F32),
        scratch_shapes=[pltpu.VMEM((tm, d), BF16), pltpu.VMEM((tm, d), F32)],
        compiler_params=_cp("arbitrary", "arbitrary", "arbitrary"), name="ffn")(x, mod, w1, w2, g, b)


O_GA, O_GB, O_Q, O_K, O_V, O_F, O_END = 0, 512, 1024, 1536, 2048, 2560, 2568
O_COLS = 2688


def _log_sigmoid(x):
    return jnp.minimum(x, 0.0) - jnp.log(1.0 + jnp.exp(-jnp.abs(x)))


def _odd_in_kernel(*refs, prompt):
    if prompt:
        (x_ref, mod_ref, w_ref, bf_ref, u_ref, q_ref, k_ref, v_ref, kb_ref, vb_ref, lf_ref, cum_ref, carry_s) = refs
    else:
        (x_ref, mod_ref, w_ref, bf_ref, u_ref, q_ref, k_ref, v_ref, kb_ref, vb_ref, lf_ref) = refs
    m = mod_ref[0]
    h = x_ref[0] * (1.0 + m[1]) + m[0]
    z = _dot(h.astype(BF16), w_ref[...])
    u_ref[0] = z[:, O_GA:O_GB] * jax.nn.sigmoid(z[:, O_GB:O_Q])
    q_ref[0] = (z[:, O_Q:O_K] * FOX_SCALE).astype(BF16)
    k = z[:, O_K:O_V]
    v = z[:, O_V:O_F]
    k_ref[0] = k
    v_ref[0] = v
    kb_ref[0] = k.astype(BF16)
    vb_ref[0] = v.astype(BF16)
    lf = _log_sigmoid(z[:, O_F:O_END] + bf_ref[...])
    lf_ref[0] = lf
    if prompt:
        @pl.when(pl.program_id(1) == 0)
        def _reset():
            carry_s[...] = jnp.zeros_like(carry_s)

        tm = lf.shape[0]
        tri = (lax.broadcasted_iota(jnp.int32, (tm, tm), 0) >= lax.broadcasted_iota(jnp.int32, (tm, tm), 1))
        tri = jnp.where(tri, 1.0, 0.0).astype(BF16)
        cum = sum(_dot(tri, piece) for piece in _split3(lf)) + carry_s[...]
        cum_ref[0] = cum
        carry_s[...] = cum[tm - 1:tm]


def _odd_in(x, mod, w_in, bf, tm, prompt):
    nb, t, d = x.shape
    r = mod.shape[2]
    cw = O_GB - O_GA
    dw = O_K - O_Q
    outs = [(cw, F32), (dw, BF16), (dw, F32), (dw, F32), (dw, BF16), (dw, BF16), (D_HEADS, F32)]
    scratch = []
    if prompt:
        outs.append((D_HEADS, F32))
        scratch = [pltpu.VMEM((1, D_HEADS), F32)]
    return pl.pallas_call(
        functools.partial(_odd_in_kernel, prompt=prompt), grid=(nb, t // tm),
        in_specs=[_row_spec(tm, d), _mod_spec(r, d), _full_spec(w_in.shape), _full_spec(bf.shape)],
        out_specs=[_row_spec(tm, c) for c, _ in outs],
        out_shape=[jax.ShapeDtypeStruct((nb, t, c), dt) for c, dt in outs],
        scratch_shapes=scratch, compiler_params=_cp("arbitrary", "arbitrary"), name="odd_in")(x, mod, w_in, bf)


def _fox_attn_kernel(qi_ref, kj_ref, last_ref, q_ref, k_ref, v_ref, ft_ref, o_ref, m_s, l_s, acc_s, *, tq, tk):
    step = pl.program_id(1)
    qi = qi_ref[step]
    kj = kj_ref[step]

    @pl.when(kj == 0)
    def _init():
        m_s[...] = jnp.full_like(m_s, NEG)
        l_s[...] = jnp.zeros_like(l_s)
        acc_s[...] = jnp.zeros_like(acc_s)

    def update(masked):
        def head(h, carry):
            v = v_ref[0, h]
            s = _dot_nt(q_ref[0, h], k_ref[0, h]) - ft_ref[0, h]
            if masked:
                rows = lax.broadcasted_iota(jnp.int32, s.shape, 0)
                cols = lax.broadcasted_iota(jnp.int32, s.shape, 1)
                s = jnp.where(rows - cols >= kj * tk - qi * tq, s, NEG)
            m_prev = m_s[h]
            m_new = jnp.maximum(m_prev, jnp.max(s, -1, keepdims=True))
            alpha = jnp.exp(m_prev - m_new)
            p = jnp.exp(s - m_new)
            l_s[h] = alpha * l_s[h] + jnp.sum(p, -1, keepdims=True)
            acc_s[h] = alpha * acc_s[h] + _dot(p.astype(BF16), v)
            m_s[h] = m_new
            return carry

        lax.fori_loop(0, D_HEADS, head, 0)

    needs_mask = (kj + 1) * tk - 1 > qi * tq

    @pl.when(needs_mask)
    def _masked():
        update(True)

    @pl.when(jnp.logical_not(needs_mask))
    def _plain():
        update(False)

    @pl.when(last_ref[step] == 1)
    def _fin():
        o_ref[0] = (acc_s[...] / l_s[...]).astype(o_ref.dtype)


def _fox_attn(q, k, v, ft, tq, tk):
    nb, nh, t, hd = q.shape
    qi, kj, last = _causal_pairs(t, tq, tk)
    qmap = lambda b, s, qi, kj, la: (b, 0, qi[s], 0)
    kmap = lambda b, s, qi, kj, la: (b, 0, kj[s], 0)
    grid_spec = pltpu.PrefetchScalarGridSpec(
        num_scalar_prefetch=3, grid=(nb, len(qi)),
        in_specs=[pl.BlockSpec((1, nh, tq, hd), qmap), pl.BlockSpec((1, nh, tk, hd), kmap),
                  pl.BlockSpec((1, nh, tk, hd), kmap),
                  pl.BlockSpec((1, nh, 1, tk), lambda b, s, qi, kj, la: (b, 0, 0, kj[s]))],
        out_specs=pl.BlockSpec((1, nh, tq, hd), qmap),
        scratch_shapes=[pltpu.VMEM((nh, tq, 1), F32), pltpu.VMEM((nh, tq, 1), F32), pltpu.VMEM((nh, tq, hd), F32)])
    return pl.pallas_call(
        functools.partial(_fox_attn_kernel, tq=tq, tk=tk), grid_spec=grid_spec,
        out_shape=jax.ShapeDtypeStruct(q.shape, BF16),
        compiler_params=_cp("arbitrary", "arbitrary"), name="fox_attn")(
            jnp.asarray(qi), jnp.asarray(kj), jnp.asarray(last), q, k, v, ft)


def _odd_out_kernel(*refs, alpha, sample):
    if sample:
        (u_ref, hist_ref, o_in_ref, x_ref, mod_ref, cw_ref, cb_ref, cg_ref, cbb_ref, wa_ref, wb_ref, g_ref, b_ref,
         o_ref) = refs
        conv = _conv_sample(u_ref[0], hist_ref, cw_ref, C_CONV)
    else:
        (u_ref, halo_ref, o_in_ref, x_ref, mod_ref, cw_ref, cb_ref, cg_ref, cbb_ref, wa_ref, wb_ref, g_ref, b_ref,
         o_ref, ext_s) = refs
        conv = _conv_prompt(u_ref[0], halo_ref, ext_s, cw_ref, pl.program_id(1), C_CONV)
    y = _ln(conv + cb_ref[...], cg_ref[...], cbb_ref[...])
    yc = y * jax.nn.sigmoid(y)
    mix = _dot(yc.astype(BF16), wa_ref[...]) + _dot(o_in_ref[0], wb_ref[...])
    m = mod_ref[0]
    o_ref[0] = _ln(alpha * x_ref[0] + (1.0 + m[2]) * mix, g_ref[...], b_ref[...])


def _qlat_kernel(qn_ref, wkn_ref, o_ref):
    qn = qn_ref[...]
    for h in range(B_HEADS):
        o_ref[h] = (_dot(qn[:, h * B_NOPE:(h + 1) * B_NOPE], wkn_ref[h]) * MLA_SCALE).astype(BF16)


def _qlat(qnope, wkn):
    n = qnope.shape[0]
    return pl.pallas_call(_qlat_kernel, out_shape=jax.ShapeDtypeStruct((B_HEADS, n, wkn.shape[-1]), BF16),
                          name="qlat")(qnope, wkn)


def _yb_kernel(o_ref, wkv_ref, y_ref):
    ys = [_dot(o_ref[h].astype(BF16), wkv_ref[h]) for h in range(B_HEADS)]
    y_ref[...] = jnp.concatenate(ys, axis=1).astype(y_ref.dtype)


def _yb(o_lat_t, wkv):
    n = o_lat_t.shape[1]
    return pl.pallas_call(_yb_kernel, out_shape=jax.ShapeDtypeStruct((n, B_HEADS * B_V), BF16),
                          name="yb")(o_lat_t, wkv)


def _online_update(s, m_s, l_s, acc_s, values):
    m_prev = m_s[...]
    m_new = jnp.maximum(m_prev, jnp.max(s, -1, keepdims=True))
    alpha = jnp.exp(m_prev - m_new)
    p = jnp.exp(s - m_new)
    l_s[...] = alpha * l_s[...] + jnp.sum(p, -1, keepdims=True)
    acc_s[...] = alpha * acc_s[...] + _dot(p.astype(BF16), values)
    m_s[...] = m_new


def _mla_dec_kernel(pt_ref, ql_ref, qr_ref, cn_ref, kn_ref, *rest, npg, page):
    lat_refs, kr_refs = rest[:npg], rest[npg:2 * npg]
    o_ref, lat_s, kr_s, m_s, l_s, acc_s = rest[2 * npg:]
    c = pl.program_id(1)

    @pl.when(c == 0)
    def _init():
        m_s[...] = jnp.full_like(m_s, NEG)
        l_s[...] = jnp.zeros_like(l_s)
        acc_s[...] = jnp.zeros_like(acc_s)

    for i in range(npg):
        lat_s[i * page:(i + 1) * page, :] = lat_refs[i][0, 0].astype(BF16)
        kr_s[i * page:(i + 1) * page, :] = kr_refs[i][0, 0].astype(BF16)
    ql = ql_ref[0]
    qr = qr_ref[0]
    s = _dot_nt(ql, lat_s[...]) + _dot_nt(qr, kr_s[...])
    _online_update(s, m_s, l_s, acc_s, lat_s[...])

    @pl.when(c == pl.num_programs(1) - 1)
    def _fin():
        cn = cn_ref[0].astype(BF16).astype(F32)
        kn = kn_ref[0].astype(BF16).astype(F32)
        s_new = (jnp.sum(ql.astype(F32) * cn, -1, keepdims=True) + jnp.sum(qr.astype(F32) * kn, -1, keepdims=True))
        m_prev = m_s[...]
        m_new = jnp.maximum(m_prev, s_new)
        alpha = jnp.exp(m_prev - m_new)
        p_new = jnp.exp(s_new - m_new)
        o_ref[0] = (alpha * acc_s[...] + p_new * cn) / (alpha * l_s[...] + p_new)


def _page_map(layer, n_pages, npg, i):
    return lambda b, c, pt: (layer, pt[b * n_pages + c * npg + i], 0, 0)


def _mla_dec(pt_flat, ql, qr, cn, kn, lat_pool, kr_pool, layer, n_pages, npg):
    ns = ql.shape[0]
    page, rank = lat_pool.shape[2], lat_pool.shape[3]
    seq = lambda shape: pl.BlockSpec((1,) + shape, lambda b, c, pt: (b, 0, 0))
    grid_spec = pltpu.PrefetchScalarGridSpec(
        num_scalar_prefetch=1, grid=(ns, n_pages // npg),
        in_specs=[seq((B_HEADS, rank)), seq((B_HEADS, B_ROPE)), seq((1, rank)), seq((1, B_ROPE))]
        + [pl.BlockSpec((1, 1, page, rank), _page_map(layer, n_pages, npg, i)) for i in range(npg)]
        + [pl.BlockSpec((1, 1, page, B_ROPE), _page_map(layer, n_pages, npg, i)) for i in range(npg)],
        out_specs=seq((B_HEADS, rank)),
        scratch_shapes=[pltpu.VMEM((npg * page, rank), BF16), pltpu.VMEM((npg * page, B_ROPE), BF16),
                        pltpu.VMEM((B_HEADS, 1), F32), pltpu.VMEM((B_HEADS, 1), F32),
                        pltpu.VMEM((B_HEADS, rank), F32)])
    return pl.pallas_call(
        functools.partial(_mla_dec_kernel, npg=npg, page=page), grid_spec=grid_spec,
        out_shape=jax.ShapeDtypeStruct((ns, B_HEADS, rank), F32),
        compiler_params=_cp("arbitrary", "arbitrary"), name="mla_dec")(
            pt_flat, ql, qr, cn, kn, *([lat_pool] * npg), *([kr_pool] * npg))


def _fox_dec_kernel(pt_ref, q_ref, kn_ref, vn_ref, lfn_ref, *rest, npg, page):
    k_refs, v_refs, lf_refs = rest[:npg], rest[npg:2 * npg], rest[2 * npg:3 * npg]
    o_ref, k_s, v_s, m_s, l_s, acc_s, carry_s = rest[3 * npg:]
    c = pl.program_id(1)
    nh = D_HEADS
    width = nh * D_HEAD_DIM

    @pl.when(c == 0)
    def _init():
        m_s[...] = jnp.full_like(m_s, NEG)
        l_s[...] = jnp.zeros_like(l_s)
        acc_s[...] = jnp.zeros_like(acc_s)
        carry_s[...] = jnp.zeros_like(carry_s)

    for i in range(npg):
        k_s[i * page:(i + 1) * page, :] = k_refs[i][0, 0].astype(BF16)
        v_s[i * page:(i + 1) * page, :] = v_refs[i][0, 0].astype(BF16)

    diag = (lax.broadcasted_iota(jnp.int32, (nh, width), 1) // D_HEAD_DIM
            == lax.broadcasted_iota(jnp.int32, (nh, width), 0))
    qbd = jnp.where(diag, q_ref[0].astype(F32), 0.0)

    rows = nh * npg
    y = jnp.concatenate([lf_refs[i][0, 0] for i in range(npg)], axis=0)
    upper = (lax.broadcasted_iota(jnp.int32, (page, page), 0) <= lax.broadcasted_iota(jnp.int32, (page, page), 1))
    upper = jnp.where(upper, 1.0, 0.0).astype(BF16)
    z = sum(_dot(piece, upper) for piece in _split3(y))
    r0 = lax.broadcasted_iota(jnp.int32, (rows, rows), 0)
    r1 = lax.broadcasted_iota(jnp.int32, (rows, rows), 1)
    earlier = jnp.where(((r0 % nh) == (r1 % nh)) & (r1 // nh < r0 // nh), 1.0, 0.0).astype(BF16)
    totals = jnp.broadcast_to(z[:, page - 1:page], (rows, page))
    off = sum(_dot(earlier, piece) for piece in _split3(totals))
    cum = z + off + jnp.concatenate([carry_s[...]] * npg, axis=0)
    carry_s[...] = cum[rows - nh:rows, page - 1:page]
    bias = jnp.concatenate([cum[nh * i:nh * (i + 1), :] for i in range(npg)], axis=1)

    s = _dot_nt(qbd.astype(BF16), k_s[...]) - bias
    _online_update(s, m_s, l_s, acc_s, v_s[...])

    @pl.when(c == pl.num_programs(1) - 1)
    def _fin():
        kn = kn_ref[0].astype(BF16).astype(F32)
        vn = vn_ref[0].astype(BF16).astype(F32)
        s_new = jnp.sum(qbd * kn, -1, keepdims=True) - (carry_s[...] + lfn_ref[0])
        m_prev = m_s[...]
        m_new = jnp.maximum(m_prev, s_new)
        alpha = jnp.exp(m_prev - m_new)
        p_new = jnp.exp(s_new - m_new)
        o = (alpha * acc_s[...] + p_new * vn) / (alpha * l_s[...] + p_new)
        o_ref[0] = jnp.sum(jnp.where(diag, o, 0.0), axis=0, keepdims=True)


def _fox_dec(pt_flat, q, kn, vn, lfn, k_pool, v_pool, lft_pool, layer, n_pages, npg):
    ns = q.shape[0]
    page, width = k_pool.shape[2], k_pool.shape[3]
    seq = lambda shape: pl.BlockSpec((1,) + shape, lambda b, c, pt: (b, 0, 0))
    pages = lambda shape: [pl.BlockSpec((1, 1) + shape, _page_map(layer, n_pages, npg, i)) for i in range(npg)]
    grid_spec = pltpu.PrefetchScalarGridSpec(
        num_scalar_prefetch=1, grid=(ns, n_pages // npg),
        in_specs=[seq((1, width)), seq((1, width)), seq((1, width)), seq((D_HEADS, 1))]
        + pages((page, width)) + pages((page, width)) + pages((D_HEADS, page)),
        out_specs=seq((1, width)),
        scratch_shapes=[pltpu.VMEM((npg * page, width), BF16), pltpu.VMEM((npg * page, width), BF16),
                        pltpu.VMEM((D_HEADS, 1), F32), pltpu.VMEM((D_HEADS, 1), F32),
                        pltpu.VMEM((D_HEADS, width), F32), pltpu.VMEM((D_HEADS, 1), F32)])
    return pl.pallas_call(
        functools.partial(_fox_dec_kernel, npg=npg, page=page), grid_spec=grid_spec,
        out_shape=jax.ShapeDtypeStruct((ns, 1, width), F32),
        compiler_params=_cp("arbitrary", "arbitrary"), name="fox_dec")(
            pt_flat, q, kn, vn, lfn, *([k_pool] * npg), *([v_pool] * npg), *([lft_pool] * npg))


def _rope_tables(pos):
    half = B_ROPE // 2
    inv = ROPE_THETA ** (-jnp.arange(half, dtype=F32) / half)
    ang = pos.astype(F32)[:, None] * inv
    cos = jnp.concatenate([jnp.cos(ang)] * 2, axis=-1)
    sin = jnp.concatenate([jnp.sin(ang)] * 2, axis=-1)
    return cos, sin, jnp.tile(cos, (1, B_HEADS)), jnp.tile(sin, (1, B_HEADS))


def _rot_cols(w):
    half = w.shape[-1] // 2
    return jnp.concatenate([-w[..., half:], w[..., :half]], axis=-1)


def _prep_even(w_in, w_q_b, w_kv_b):
    d = w_in.shape[0]
    kr = w_in[:, E_KR:E_KRR]
    w_in_p = jnp.concatenate([w_in, _rot_cols(kr), jnp.zeros((d, E_COLS - E_END), w_in.dtype)], axis=1)
    qr = w_q_b.shape[0]
    wq = w_q_b.reshape(qr, B_HEADS, B_NOPE + B_ROPE)
    rp = wq[:, :, B_NOPE:]
    wq_p = jnp.concatenate([wq[:, :, :B_NOPE].reshape(qr, -1), rp.reshape(qr, -1), _rot_cols(rp).reshape(qr, -1)],
                           axis=1)
    wkn = jnp.transpose(w_kv_b[:, :, :B_NOPE], (1, 2, 0))
    wkv = jnp.transpose(w_kv_b[:, :, B_NOPE:], (1, 0, 2))
    return w_in_p.astype(BF16), wq_p.astype(BF16), wkn.astype(BF16), wkv.astype(BF16)


def _heads_major(a):
    nb, t, _ = a.shape
    return jnp.transpose(a.reshape(nb, t, D_HEADS, D_HEAD_DIM), (0, 2, 1, 3))


def kernel(x_prompt, x_sample, cache_mla_latent, cache_mla_krope, state_conv_a, cache_fox_k, cache_fox_v,
           cache_fox_logf, state_conv_c, page_table, c_prompt, c_sample, w_ada, b_ada, ln_g, ln_b,
           w_in_even, w_q_b, w_kv_b, q_norm, kv_norm, conv_a, w_out_even, w_in_odd, b_f, conv_c,
           conv_c_b, cn_g, cn_b, w_out_odd, w_ff1, w_ff2):
    depth = w_ada.shape[0]
    alpha = float((2 * depth) ** 0.25)
    nbp, seq, d = x_prompt.shape
    ns = x_sample.shape[0]
    n_pages = page_table.shape[1]
    page = cache_mla_latent.shape[2]
    past_len = n_pages * page
    n_pool = cache_fox_k.shape[1]
    aw = conv_a.shape[-1]
    cw = conv_c.shape[-1]

    tm = min(256, seq)
    tq_mla, tk_mla = min(256, seq), min(512, seq)
    tq_fox, tk_fox = min(512, seq), min(512, seq)
    tm_ffn, tf = min(1024, seq), min(1024, w_ff1.shape[-1])
    npg = min(16, n_pages)

    n_c = nbp + ns
    n_pad = -n_c % 8
    c_all = jnp.concatenate([c_prompt, c_sample, jnp.zeros((n_pad, d), F32)], axis=0)
    mods = _ada(c_all, w_ada, b_ada)
    mod_p = jnp.transpose(mods[:, :, :nbp], (0, 2, 1, 3))[:, :, :, None, :]
    mod_s = mods[:, None, :, nbp:n_c]

    tabs_p = _rope_tables(jnp.arange(seq, dtype=jnp.int32))
    tabs_s = _rope_tables(jnp.full((ns,), past_len, dtype=jnp.int32))
    pt_flat = page_table.reshape(-1)
    k_pool = cache_fox_k.reshape(cache_fox_k.shape[:3] + (-1,))
    v_pool = cache_fox_v.reshape(cache_fox_v.shape[:3] + (-1,))
    lft_pool = jnp.swapaxes(cache_fox_logf, 2, 3)

    xp = x_prompt
    xs = x_sample.reshape(1, ns, d)
    even_p, even_s, odd_p, odd_s = [], [], [], []
    for l in range(depth):
        i = l // 2
        row = lambda a: a.reshape(1, -1)
        g0, b0, g1, b1 = row(ln_g[l, 0]), row(ln_b[l, 0]), row(ln_g[l, 1]), row(ln_b[l, 1])
        if l % 2 == 0:
            w_in_p, wq_p, wkn, wkv = _prep_even(w_in_even[i], w_q_b[i], w_kv_b[i])
            w_out = w_out_even[i].astype(BF16)
            tail = (conv_a[i], w_out[:aw], w_out[aw:], g0, b0)
            qn, kvn = row(q_norm[i]), row(kv_norm[i])
            u, gb, ckv, ckvb, kr, krb, qnope, qrope = _even_in(xp, mod_p[l], w_in_p, qn, kvn, wq_p, *tabs_p, tm)
            yb = _mla_attn(qnope, qrope, ckvb, krb, wkn, wkv, tq_mla, tk_mla)
            xp = _mixer_out(_even_out_kernel, u, None, (gb, yb), xp, mod_p[l], tail, tm, alpha, False, A_CONV)
            even_p.append((ckv, kr, u[:, seq - (A_CONV - 1):]))
            u, gb, ckv, ckvb, kr, krb, qnope, qrope = _even_in(xs, mod_s[l], w_in_p, qn, kvn, wq_p, *tabs_s, ns)
            ql = jnp.transpose(_qlat(qnope[0], wkn), (1, 0, 2))
            qr = qrope[0].reshape(ns, B_HEADS, B_ROPE)
            o_lat = _mla_dec(pt_flat, ql, qr, ckv[0][:, None], kr[0][:, None], cache_mla_latent, cache_mla_krope,
                             i, n_pages, npg)
            yb = _yb(jnp.transpose(o_lat, (1, 0, 2)), wkv)[None]
            hist = jnp.transpose(state_conv_a[i], (1, 0, 2))
            xs = _mixer_out(_even_out_kernel, u, hist, (gb, yb), xs, mod_s[l], tail, ns, alpha, True, A_CONV)
            even_s.append((ckv[0][:, None], kr[0][:, None],
                           jnp.concatenate([state_conv_a[i][:, 1:], u[0][:, None]], axis=1)))
        else:
            w_in_p = jnp.concatenate([w_in_odd[i], jnp.zeros((d, O_COLS - O_END), F32)], axis=1).astype(BF16)
            w_out = w_out_odd[i].astype(BF16)
            tail = (conv_c[i], row(conv_c_b[i]), row(cn_g[i]), row(cn_b[i]), w_out[:cw], w_out[cw:], g0, b0)
            bf = row(b_f[i])
            u, q, k, v, kb, vb, lf, cum = _odd_in(xp, mod_p[l], w_in_p, bf, tm, True)
            ft = jnp.transpose(cum, (0, 2, 1))[:, :, None, :]
            o = _fox_attn(_heads_major(q), _heads_major(kb), _heads_major(vb), ft, tq_fox, tk_fox)
            o = jnp.transpose(o, (0, 2, 1, 3)).reshape(nbp, seq, -1)
            xp = _mixer_out(_odd_out_kernel, u, None, (o,), xp, mod_p[l], tail, tm, alpha, False, C_CONV)
            hd = (nbp, seq, D_HEADS, D_HEAD_DIM)
            odd_p.append((k.reshape(hd), v.reshape(hd), lf, u[:, seq - (C_CONV - 1):]))
            u, q, k, v, kb, vb, lf = _odd_in(xs, mod_s[l], w_in_p, bf, ns, False)
            o = _fox_dec(pt_flat, q[0][:, None], k[0][:, None], v[0][:, None], lf[0][:, :, None],
                         k_pool, v_pool, lft_pool, i, n_pages, npg)
            o = o.reshape(1, ns, -1).astype(BF16)
            hist = jnp.transpose(state_conv_c[i], (1, 0, 2))
            xs = _mixer_out(_odd_out_kernel, u, hist, (o,), xs, mod_s[l], tail, ns, alpha, True, C_CONV)
            hd = (ns, 1, D_HEADS, D_HEAD_DIM)
            odd_s.append((k[0].reshape(hd), v[0].reshape(hd), lf[0][:, None],
                          jnp.concatenate([state_conv_c[i][:, 1:], u[0][:, None]], axis=1)))
        w1, w2 = w_ff1[l].astype(BF16), w_ff2[l].astype(BF16)
        xp = _ffn(xp, mod_p[l], w1, w2, g1, b1, tm_ffn, tf, alpha)
        xs = _ffn(xs, mod_s[l], w1, w2, g1, b1, ns, tf, alpha)

    stack = lambda states: [jnp.stack(s) for s in zip(*states)]
    lat_p, kr_p, ca_p = stack(even_p)
    k_p, v_p, lf_p, cc_p = stack(odd_p)
    lat_s, kr_s, ca_s = stack(even_s)
    k_s, v_s, lf_s, cc_s = stack(odd_s)
    return (xp, xs.reshape(ns, 1, d), lat_p, kr_p, ca_p, k_p, v_p, lf_p, cc_p,
            lat_s, kr_s, ca_s, k_s, v_s, lf_s, cc_s)
```

```python
import functools

import numpy as np
import jax
import jax.numpy as jnp
from jax import lax
from jax.experimental import pallas as pl
from jax.experimental.pallas import tpu as pltpu

F32 = jnp.float32
BF16 = jnp.bfloat16

A_CONV = 3
C_CONV = 31
B_HEADS = 8
B_NOPE = 64
B_ROPE = 32
B_V = 64
D_HEADS = 8
D_HEAD_DIM = 64
ROPE_THETA = 10000.0
MLA_SCALE = (B_NOPE + B_ROPE) ** -0.5
FOX_SCALE = D_HEAD_DIM ** -0.5
LN_EPS = 1e-5
RMS_EPS = 1e-6
NEG = -1e30
LANE = 128
VMEM_LIMIT = 56 * 1024 * 1024

_NT = (((1,), (1,)), ((), ()))


def _cp(*sem):
    return pltpu.CompilerParams(dimension_semantics=sem, vmem_limit_bytes=VMEM_LIMIT)


def _ln(y, g, b):
    mu = jnp.mean(y, -1, keepdims=True)
    d = y - mu
    var = jnp.mean(d * d, -1, keepdims=True)
    return d * lax.rsqrt(var + LN_EPS) * g + b


def _rms(y, g):
    return y * lax.rsqrt(jnp.mean(y * y, -1, keepdims=True) + RMS_EPS) * g


def _split3(x):
    hi = x.astype(BF16)
    r = x - hi.astype(F32)
    mid = r.astype(BF16)
    lo = (r - mid.astype(F32)).astype(BF16)
    return hi, mid, lo


def _dot(a, b):
    return jnp.dot(a, b, preferred_element_type=F32)


def _dot_nt(a, b):
    return lax.dot_general(a, b, _NT, preferred_element_type=F32)


def _mod_spec(r, d):
    if r == 1:
        return pl.BlockSpec((1, 6, 1, d), lambda b, i, *_: (b, 0, 0, 0))
    return pl.BlockSpec((1, 6, r, d), lambda b, i, *_: (b, 0, i, 0))


def _row_spec(tm, c):
    return pl.BlockSpec((1, tm, c), lambda b, i, *_: (b, i, 0))


def _time_minor_spec(c, tm):
    return pl.BlockSpec((1, c, tm), lambda b, i, *_: (b, 0, i))


def _full_spec(shape):
    return pl.BlockSpec(shape, lambda *_: (0,) * len(shape))


def _ada_kernel(c_ref, w_ref, b_ref, o_ref):
    c = c_ref[...]
    s = (c * jax.nn.sigmoid(c)).astype(BF16)
    o_ref[0, 0] = _dot(s, w_ref[0].astype(BF16)) + b_ref[0]


def _ada(c_all, w_ada, b_ada):
    depth, d, d6 = w_ada.shape
    n = c_all.shape[0]
    return pl.pallas_call(
        _ada_kernel, grid=(depth, d6 // d),
        in_specs=[pl.BlockSpec((n, d), lambda l, j: (0, 0)),
                  pl.BlockSpec((1, d, d), lambda l, j: (l, 0, j)),
                  pl.BlockSpec((1, 1, d), lambda l, j: (l, 0, j))],
        out_specs=pl.BlockSpec((1, 1, n, d), lambda l, j: (l, j, 0, 0)),
        out_shape=jax.ShapeDtypeStruct((depth, d6 // d, n, d), F32),
        compiler_params=_cp("arbitrary", "arbitrary"), name="ada")(c_all, w_ada, b_ada.reshape(depth, 1, d6))


E_XA, E_GB, E_GC, E_QA, E_KV, E_KR, E_KRR, E_END = 0, 512, 1024, 1536, 1920, 2176, 2208, 2240
E_COLS = 2304
Q_NOPE, Q_ROPE, Q_ROT, Q_END = 0, 512, 768, 1024
MLA_RANK = E_KR - E_KV
MLA_KC = 384


def _even_in_kernel(x_ref, mod_ref, w_ref, qn_ref, kvn_ref, wq_ref, ck_ref, sk_ref, cq_ref, sq_ref,
                    u_ref, gb_ref, ckv_ref, kr_ref, kcat_ref, qnope_ref, qrope_ref):
    m = mod_ref[0]
    h = x_ref[0] * (1.0 + m[1]) + m[0]
    z = _dot(h.astype(BF16), w_ref[...])
    u_ref[0] = z[:, E_GC:E_QA] * z[:, E_XA:E_GB]
    gb_ref[0] = z[:, E_GB:E_GC]
    ckv = _rms(z[:, E_KV:E_KR], kvn_ref[...])
    ckv_ref[0] = ckv
    kr = z[:, E_KR:E_KRR] * ck_ref[...] + z[:, E_KRR:E_END] * sk_ref[...]
    kr_ref[0] = kr
    kcat_ref[0, :, 0:MLA_RANK] = ckv.astype(BF16)
    kcat_ref[0, :, MLA_RANK:MLA_RANK + B_ROPE] = kr.astype(BF16)
    kcat_ref[0, :, MLA_RANK + B_ROPE:] = jnp.zeros((kr.shape[0], MLA_KC - MLA_RANK - B_ROPE), BF16)
    qn = _rms(z[:, E_QA:E_KV], qn_ref[...])
    q = _dot(qn.astype(BF16), wq_ref[...])
    qnope_ref[0] = q[:, Q_NOPE:Q_ROPE].astype(BF16)
    qr = q[:, Q_ROPE:Q_ROT] * cq_ref[...] + q[:, Q_ROT:Q_END] * sq_ref[...]
    qrope_ref[0] = (qr * MLA_SCALE).astype(BF16)


def _even_in(x, mod, w_in, qn, kvn, wq, ck, sk, cq, sq, tm):
    nb, t, d = x.shape
    r = mod.shape[2]
    aw = E_GB - E_XA
    outs = [(aw, F32), (aw, F32), (MLA_RANK, F32), (B_ROPE, F32), (MLA_KC, BF16),
            (B_HEADS * B_NOPE, BF16), (B_HEADS * B_ROPE, BF16)]
    tab = lambda c: pl.BlockSpec((tm, c), lambda b, i: (i, 0))
    return pl.pallas_call(
        _even_in_kernel, grid=(nb, t // tm),
        in_specs=[_row_spec(tm, d), _mod_spec(r, d), _full_spec(w_in.shape), _full_spec(qn.shape),
                  _full_spec(kvn.shape), _full_spec(wq.shape), tab(B_ROPE), tab(B_ROPE),
                  tab(B_HEADS * B_ROPE), tab(B_HEADS * B_ROPE)],
        out_specs=[_row_spec(tm, c) for c, _ in outs],
        out_shape=[jax.ShapeDtypeStruct((nb, t, c), dt) for c, dt in outs],
        compiler_params=_cp("arbitrary", "arbitrary"), name="even_in")(x, mod, w_in, qn, kvn, wq, ck, sk, cq, sq)


def _causal_pairs(t, tq, tk):
    qi, kj, last = [], [], []
    for i in range(t // tq):
        n = ((i + 1) * tq - 1) // tk + 1
        for j in range(n):
            qi.append(i)
            kj.append(j)
            last.append(int(j == n - 1))
    return (np.asarray(qi, np.int32), np.asarray(kj, np.int32), np.asarray(last, np.int32))


ATTN_LOOKAHEAD = 2


def _mla_attn_kernel(qi_ref, kj_ref, last_ref, qn_ref, qr_ref, kc_ref, wkn_ref, wkv_ref, yb_ref,
                     qc_s, m_s, l_s, acc_s, *, tq, tk):
    step = pl.program_id(1)
    qi = qi_ref[step]
    kj = kj_ref[step]

    @pl.when(kj == 0)
    def _init():
        qn = qn_ref[0]
        qr = qr_ref[0]
        for h in range(B_HEADS):
            ql = _dot(qn[:, h * B_NOPE:(h + 1) * B_NOPE], wkn_ref[h]) * MLA_SCALE
            qc_s[h * tq:(h + 1) * tq, 0:MLA_RANK] = ql.astype(BF16)
            qc_s[h * tq:(h + 1) * tq, MLA_RANK:MLA_RANK + B_ROPE] = qr[:, h * B_ROPE:(h + 1) * B_ROPE]
            qc_s[h * tq:(h + 1) * tq, MLA_RANK + B_ROPE:] = jnp.zeros((tq, MLA_KC - MLA_RANK - B_ROPE), BF16)
        m_s[...] = jnp.full_like(m_s, NEG)
        l_s[...] = jnp.zeros_like(l_s)
        acc_s[...] = jnp.zeros_like(acc_s)

    def update(masked):
        kc = kc_ref[0]
        v = kc[:, :MLA_RANK]

        def scores(h):
            return _dot_nt(qc_s[h * tq:(h + 1) * tq, :], kc)

        pending = [scores(h) for h in range(ATTN_LOOKAHEAD)]
        for h in range(B_HEADS):
            rows = slice(h * tq, (h + 1) * tq)
            s = pending.pop(0)
            if h + ATTN_LOOKAHEAD < B_HEADS:
                pending.append(scores(h + ATTN_LOOKAHEAD))
            if masked:
                ri = lax.broadcasted_iota(jnp.int32, s.shape, 0)
                ci = lax.broadcasted_iota(jnp.int32, s.shape, 1)
                s = jnp.where(ri - ci >= kj * tk - qi * tq, s, NEG)
            m_prev = m_s[rows, :]
            m_new = jnp.maximum(m_prev, jnp.max(s, -1, keepdims=True))
            alpha = jnp.exp(m_prev - m_new)
            p = jnp.exp(s - m_new)
            l_s[rows, :] = alpha * l_s[rows, :] + jnp.sum(p, -1, keepdims=True)
            acc_s[rows, :] = alpha * acc_s[rows, :] + _dot(p.astype(BF16), v)
            m_s[rows, :] = m_new

    needs_mask = (kj + 1) * tk - 1 > qi * tq

    @pl.when(needs_mask)
    def _masked():
        update(True)

    @pl.when(jnp.logical_not(needs_mask))
    def _plain():
        update(False)

    @pl.when(last_ref[step] == 1)
    def _fin():
        o = acc_s[...] / l_s[...]
        ys = [_dot(o[h * tq:(h + 1) * tq].astype(BF16), wkv_ref[h]) for h in range(B_HEADS)]
        yb_ref[0] = jnp.concatenate(ys, axis=1).astype(yb_ref.dtype)


def _mla_attn(qnope, qrope, kcat, wkn, wkv, tq, tk):
    nb, t, _ = qnope.shape
    qi, kj, last = _causal_pairs(t, tq, tk)
    grid_spec = pltpu.PrefetchScalarGridSpec(
        num_scalar_prefetch=3, grid=(nb, len(qi)),
        in_specs=[pl.BlockSpec((1, tq, qnope.shape[-1]), lambda b, s, qi, kj, la: (b, qi[s], 0)),
                  pl.BlockSpec((1, tq, qrope.shape[-1]), lambda b, s, qi, kj, la: (b, qi[s], 0)),
                  pl.BlockSpec((1, tk, MLA_KC), lambda b, s, qi, kj, la: (b, kj[s], 0)),
                  _full_spec(wkn.shape), _full_spec(wkv.shape)],
        out_specs=pl.BlockSpec((1, tq, B_HEADS * B_V), lambda b, s, qi, kj, la: (b, qi[s], 0)),
        scratch_shapes=[pltpu.VMEM((B_HEADS * tq, MLA_KC), BF16),
                        pltpu.VMEM((B_HEADS * tq, 1), F32), pltpu.VMEM((B_HEADS * tq, 1), F32),
                        pltpu.VMEM((B_HEADS * tq, MLA_RANK), F32)])
    return pl.pallas_call(
        functools.partial(_mla_attn_kernel, tq=tq, tk=tk), grid_spec=grid_spec,
        out_shape=jax.ShapeDtypeStruct((nb, t, B_HEADS * B_V), BF16),
        compiler_params=_cp("arbitrary", "arbitrary"), name="mla_attn")(
            jnp.asarray(qi), jnp.asarray(kj), jnp.asarray(last), qnope, qrope, kcat, wkn, wkv)


def _conv_prompt(u, halo_ref, ext_s, w_ref, i, k):
    tm = u.shape[0]
    hr = halo_ref.shape[1]
    ext_s[0:hr] = jnp.where(i > 0, halo_ref[0], 0.0)
    ext_s[hr:hr + tm] = u
    off = hr - (k - 1)
    acc = w_ref[k - 1:k] * u
    for j in range(k - 1):
        acc = acc + w_ref[j:j + 1] * ext_s[off + j:off + j + tm]
    return acc


def _conv_sample(u, hist_ref, w_ref, k):
    acc = w_ref[k - 1:k] * u
    for j in range(k - 1):
        acc = acc + w_ref[j:j + 1] * hist_ref[j]
    return acc


def _halo_spec(tm, hr, c):
    return pl.BlockSpec((1, hr, c), lambda b, i: (b, jnp.maximum(i * (tm // hr) - 1, 0), 0))


def _even_out_kernel(*refs, alpha, sample):
    if sample:
        (u_ref, hist_ref, gb_ref, yb_ref, x_ref, mod_ref, cw_ref, wa_ref, wb_ref, g_ref, b_ref, o_ref) = refs
        conv = _conv_sample(u_ref[0], hist_ref, cw_ref, A_CONV)
    else:
        (u_ref, halo_ref, gb_ref, yb_ref, x_ref, mod_ref, cw_ref, wa_ref, wb_ref, g_ref, b_ref, o_ref, ext_s) = refs
        conv = _conv_prompt(u_ref[0], halo_ref, ext_s, cw_ref, pl.program_id(1), A_CONV)
    ya = gb_ref[0] * conv
    mix = _dot(ya.astype(BF16), wa_ref[...]) + _dot(yb_ref[0], wb_ref[...])
    m = mod_ref[0]
    o_ref[0] = _ln(alpha * x_ref[0] + (1.0 + m[2]) * mix, g_ref[...], b_ref[...])


def _mixer_out(kernel, u, side, others, other_specs, x, mod, tail, tm, alpha, sample, k):
    nb, t, d = x.shape
    c = u.shape[-1]
    r = mod.shape[2]
    hr = 8 if k - 1 <= 8 else 32
    if sample:
        side_arg, side_spec, scratch = side, _full_spec(side.shape), []
    else:
        side_arg, side_spec, scratch = u, _halo_spec(tm, hr, c), [pltpu.VMEM((tm + hr, c), F32)]
    return pl.pallas_call(
        functools.partial(kernel, alpha=alpha, sample=sample), grid=(nb, t // tm),
        in_specs=[_row_spec(tm, c), side_spec] + list(other_specs)
        + [_row_spec(tm, d), _mod_spec(r, d)] + [_full_spec(w.shape) for w in tail],
        out_specs=_row_spec(tm, d), out_shape=jax.ShapeDtypeStruct((nb, t, d), F32),
        scratch_shapes=scratch, compiler_params=_cp("arbitrary", "arbitrary"),
        name=kernel.__name__.strip("_"))(u, side_arg, *others, x, mod, *tail)


def _ffn_kernel(x_ref, mod_ref, w1_ref, w2_ref, g_ref, b_ref, o_ref, h_s, acc_s, *, alpha):
    f = pl.program_id(2)

    @pl.when(f == 0)
    def _first():
        m = mod_ref[0]
        h_s[...] = (x_ref[0] * (1.0 + m[4]) + m[3]).astype(BF16)
        acc_s[...] = jnp.zeros_like(acc_s)

    a = jnp.square(jnp.maximum(_dot(h_s[...], w1_ref[...]), 0.0))
    acc_s[...] += _dot(a.astype(BF16), w2_ref[...])

    @pl.when(f == pl.num_programs(2) - 1)
    def _last():
        m = mod_ref[0]
        o_ref[0] = _ln(alpha * x_ref[0] + (1.0 + m[5]) * acc_s[...], g_ref[...], b_ref[...])


def _ffn(x, mod, w1, w2, g, b, tm, tf, alpha):
    nb, t, d = x.shape
    r = mod.shape[2]
    ff = w1.shape[1]
    return pl.pallas_call(
        functools.partial(_ffn_kernel, alpha=alpha), grid=(nb, t // tm, ff // tf),
        in_specs=[_row_spec(tm, d), _mod_spec(r, d),
                  pl.BlockSpec((d, tf), lambda b, i, f: (0, f)), pl.BlockSpec((tf, d), lambda b, i, f: (f, 0)),
                  _full_spec(g.shape), _full_spec(b.shape)],
        out_specs=_row_spec(tm, d), out_shape=jax.ShapeDtypeStruct((nb, t, d), F32),
        scratch_shapes=[pltpu.VMEM((tm, d), BF16), pltpu.VMEM((tm, d), F32)],
        compiler_params=_cp("arbitrary", "arbitrary", "arbitrary"), name="ffn")(x, mod, w1, w2, g, b)


O_GA, O_GB, O_Q, O_K, O_V, O_F, O_END = 0, 512, 1024, 1536, 2048, 2560, 2568
O_COLS = 2688


def _log_sigmoid(x):
    return jnp.minimum(x, 0.0) - jnp.log(1.0 + jnp.exp(-jnp.abs(x)))


def _odd_in_kernel(*refs, prompt):
    if prompt:
        (x_ref, mod_ref, w_ref, bf_ref, u_ref, q_ref, k_ref, v_ref, kb_ref, vb_ref, lf_ref,
         hi_ref, mid_ref, lo_ref, carry_s) = refs
    else:
        (x_ref, mod_ref, w_ref, bf_ref, u_ref, q_ref, k_ref, v_ref, kb_ref, vb_ref, lf_ref) = refs
    m = mod_ref[0]
    h = x_ref[0] * (1.0 + m[1]) + m[0]
    z = _dot(h.astype(BF16), w_ref[...])
    u_ref[0] = z[:, O_GA:O_GB] * jax.nn.sigmoid(z[:, O_GB:O_Q])
    q_ref[0] = (z[:, O_Q:O_K] * FOX_SCALE).astype(BF16)
    k = z[:, O_K:O_V]
    v = z[:, O_V:O_F]
    k_ref[0] = k
    v_ref[0] = v
    kb_ref[0] = k.astype(BF16)
    vb_ref[0] = v.astype(BF16)
    lf = _log_sigmoid(z[:, O_F:O_END] + bf_ref[...])
    lf_ref[0] = lf
    if prompt:
        @pl.when(pl.program_id(1) == 0)
        def _reset():
            carry_s[...] = jnp.zeros_like(carry_s)

        tm = lf.shape[0]
        tri = (lax.broadcasted_iota(jnp.int32, (tm, tm), 0) >= lax.broadcasted_iota(jnp.int32, (tm, tm), 1))
        tri = jnp.where(tri, 1.0, 0.0).astype(BF16)
        cum = sum(_dot(tri, piece) for piece in _split3(lf)) + carry_s[...]
        carry_s[...] = cum[tm - 1:tm]
        hi_ref[0], mid_ref[0], lo_ref[0] = _split3(cum)


def _odd_in(x, mod, w_in, bf, tm, prompt):
    nb, t, d = x.shape
    r = mod.shape[2]
    cw = O_GB - O_GA
    dw = O_K - O_Q
    outs = [(cw, F32), (dw, BF16), (dw, F32), (dw, F32), (dw, BF16), (dw, BF16), (D_HEADS, F32)]
    scratch = []
    if prompt:
        outs += [(D_HEADS, BF16)] * 3
        scratch = [pltpu.VMEM((1, D_HEADS), F32)]
    return pl.pallas_call(
        functools.partial(_odd_in_kernel, prompt=prompt), grid=(nb, t // tm),
        in_specs=[_row_spec(tm, d), _mod_spec(r, d), _full_spec(w_in.shape), _full_spec(bf.shape)],
        out_specs=[_row_spec(tm, c) for c, _ in outs],
        out_shape=[jax.ShapeDtypeStruct((nb, t, c), dt) for c, dt in outs],
        scratch_shapes=scratch, compiler_params=_cp("arbitrary", "arbitrary"), name="odd_in")(x, mod, w_in, bf)


FOX_KC = 128


def _fox_attn_kernel(qi_ref, kj_ref, last_ref, ka_ref, qa_ref, vt_ref, o_ref, m_s, l_s, acc_s, *, tq, tk):
    step = pl.program_id(1)
    qi = qi_ref[step]
    kj = kj_ref[step]

    @pl.when(kj == 0)
    def _init():
        m_s[...] = jnp.full_like(m_s, NEG)
        l_s[...] = jnp.zeros_like(l_s)
        acc_s[...] = jnp.zeros_like(acc_s)

    def update(masked):
        def scores(h):
            return _dot(ka_ref[0, h], qa_ref[0, h])

        pending = [scores(h) for h in range(ATTN_LOOKAHEAD)]
        for h in range(D_HEADS):
            s = pending.pop(0)
            if h + ATTN_LOOKAHEAD < D_HEADS:
                pending.append(scores(h + ATTN_LOOKAHEAD))
            if masked:
                ki = lax.broadcasted_iota(jnp.int32, s.shape, 0)
                qq = lax.broadcasted_iota(jnp.int32, s.shape, 1)
                s = jnp.where(qq - ki >= kj * tk - qi * tq, s, NEG)
            m_prev = m_s[h]
            m_new = jnp.maximum(m_prev, jnp.max(s, 0, keepdims=True))
            alpha = jnp.exp(m_prev - m_new)
            p = jnp.exp(s - m_new)
            l_s[h] = alpha * l_s[h] + jnp.sum(p, 0, keepdims=True)
            acc_s[h] = alpha * acc_s[h] + _dot(vt_ref[0, h], p.astype(BF16))
            m_s[h] = m_new

    needs_mask = (kj + 1) * tk - 1 > qi * tq

    @pl.when(needs_mask)
    def _masked():
        update(True)

    @pl.when(jnp.logical_not(needs_mask))
    def _plain():
        update(False)

    @pl.when(last_ref[step] == 1)
    def _fin():
        o_ref[0] = (acc_s[...] / l_s[...]).astype(o_ref.dtype)


def _fox_attn(ka, qa_t, v_t, tq, tk):
    nb, nh, t, kc = ka.shape
    hd = v_t.shape[2]
    qi, kj, last = _causal_pairs(t, tq, tk)
    qmap = lambda b, s, qi, kj, la: (b, 0, 0, qi[s])
    grid_spec = pltpu.PrefetchScalarGridSpec(
        num_scalar_prefetch=3, grid=(nb, len(qi)),
        in_specs=[pl.BlockSpec((1, nh, tk, kc), lambda b, s, qi, kj, la: (b, 0, kj[s], 0)),
                  pl.BlockSpec((1, nh, kc, tq), qmap),
                  pl.BlockSpec((1, nh, hd, tk), lambda b, s, qi, kj, la: (b, 0, 0, kj[s]))],
        out_specs=pl.BlockSpec((1, nh, hd, tq), qmap),
        scratch_shapes=[pltpu.VMEM((nh, 1, tq), F32), pltpu.VMEM((nh, 1, tq), F32), pltpu.VMEM((nh, hd, tq), F32)])
    return pl.pallas_call(
        functools.partial(_fox_attn_kernel, tq=tq, tk=tk), grid_spec=grid_spec,
        out_shape=jax.ShapeDtypeStruct((nb, nh, hd, t), BF16),
        compiler_params=_cp("arbitrary", "arbitrary"), name="fox_attn")(
            jnp.asarray(qi), jnp.asarray(kj), jnp.asarray(last), ka, qa_t, v_t)


def _odd_out_kernel(*refs, alpha, sample):
    if sample:
        (u_ref, hist_ref, o_in_ref, x_ref, mod_ref, cw_ref, cb_ref, cg_ref, cbb_ref, wa_ref, wb_ref, g_ref, b_ref,
         o_ref) = refs
        conv = _conv_sample(u_ref[0], hist_ref, cw_ref, C_CONV)
    else:
        (u_ref, halo_ref, o_in_ref, x_ref, mod_ref, cw_ref, cb_ref, cg_ref, cbb_ref, wa_ref, wb_ref, g_ref, b_ref,
         o_ref, ext_s) = refs
        conv = _conv_prompt(u_ref[0], halo_ref, ext_s, cw_ref, pl.program_id(1), C_CONV)
    y = _ln(conv + cb_ref[...], cg_ref[...], cbb_ref[...])
    yc = y * jax.nn.sigmoid(y)
    attn = lax.dot_general(o_in_ref[0], wb_ref[...], (((0,), (0,)), ((), ())), preferred_element_type=F32)
    mix = _dot(yc.astype(BF16), wa_ref[...]) + attn
    m = mod_ref[0]
    o_ref[0] = _ln(alpha * x_ref[0] + (1.0 + m[2]) * mix, g_ref[...], b_ref[...])


def _qlat_kernel(qn_ref, wkn_ref, o_ref):
    qn = qn_ref[...]
    for h in range(B_HEADS):
        o_ref[h] = (_dot(qn[:, h * B_NOPE:(h + 1) * B_NOPE], wkn_ref[h]) * MLA_SCALE).astype(BF16)


def _qlat(qnope, wkn):
    n = qnope.shape[0]
    return pl.pallas_call(_qlat_kernel, out_shape=jax.ShapeDtypeStruct((B_HEADS, n, wkn.shape[-1]), BF16),
                          name="qlat")(qnope, wkn)


def _yb_kernel(o_ref, wkv_ref, y_ref):
    ys = [_dot(o_ref[h].astype(BF16), wkv_ref[h]) for h in range(B_HEADS)]
    y_ref[...] = jnp.concatenate(ys, axis=1).astype(y_ref.dtype)


def _yb(o_lat_t, wkv):
    n = o_lat_t.shape[1]
    return pl.pallas_call(_yb_kernel, out_shape=jax.ShapeDtypeStruct((n, B_HEADS * B_V), BF16),
                          name="yb")(o_lat_t, wkv)


def _online_update(s, m_s, l_s, acc_s, values, values_transposed):
    m_prev = m_s[...]
    m_new = jnp.maximum(m_prev, jnp.max(s, -1, keepdims=True))
    alpha = jnp.exp(m_prev - m_new)
    p = jnp.exp(s - m_new)
    l_s[...] = alpha * l_s[...] + jnp.sum(p, -1, keepdims=True)
    pv = _dot_nt(p.astype(BF16), values) if values_transposed else _dot(p.astype(BF16), values)
    acc_s[...] = alpha * acc_s[...] + pv
    m_s[...] = m_new


def _mla_dec_kernel(pt_ref, ql_ref, qr_ref, cn_ref, kn_ref, *rest, npg, page):
    lat_refs, kr_refs = rest[:npg], rest[npg:2 * npg]
    o_ref, lat_s, kr_s, m_s, l_s, acc_s = rest[2 * npg:]
    c = pl.program_id(1)

    @pl.when(c == 0)
    def _init():
        m_s[...] = jnp.full_like(m_s, NEG)
        l_s[...] = jnp.zeros_like(l_s)
        acc_s[...] = jnp.zeros_like(acc_s)

    for i in range(npg):
        lat_s[i * page:(i + 1) * page, :] = lat_refs[i][0, 0].astype(BF16)
        kr_s[:, i * page:(i + 1) * page] = kr_refs[i][0, 0].astype(BF16)
    ql = ql_ref[0]
    qr = qr_ref[0]
    s = _dot_nt(ql, lat_s[...]) + _dot(qr, kr_s[...])
    _online_update(s, m_s, l_s, acc_s, lat_s[...], False)

    @pl.when(c == pl.num_programs(1) - 1)
    def _fin():
        cn = cn_ref[0].astype(BF16).astype(F32)
        kn = kn_ref[0].astype(BF16).astype(F32)
        s_new = (jnp.sum(ql.astype(F32) * cn, -1, keepdims=True) + jnp.sum(qr.astype(F32) * kn, -1, keepdims=True))
        m_prev = m_s[...]
        m_new = jnp.maximum(m_prev, s_new)
        alpha = jnp.exp(m_prev - m_new)
        p_new = jnp.exp(s_new - m_new)
        o_ref[0] = (alpha * acc_s[...] + p_new * cn) / (alpha * l_s[...] + p_new)


def _page_map(layer, n_pages, npg, i):
    return lambda b, c, pt: (layer, pt[b * n_pages + c * npg + i], 0, 0)


def _mla_dec(pt_flat, ql, qr, cn, kn, lat_pool, krt_pool, layer, n_pages, npg):
    ns = ql.shape[0]
    page, rank = lat_pool.shape[2], lat_pool.shape[3]
    seq = lambda shape: pl.BlockSpec((1,) + shape, lambda b, c, pt: (b, 0, 0))
    grid_spec = pltpu.PrefetchScalarGridSpec(
        num_scalar_prefetch=1, grid=(ns, n_pages // npg),
        in_specs=[seq((B_HEADS, rank)), seq((B_HEADS, B_ROPE)), seq((1, rank)), seq((1, B_ROPE))]
        + [pl.BlockSpec((1, 1, page, rank), _page_map(layer, n_pages, npg, i)) for i in range(npg)]
        + [pl.BlockSpec((1, 1, B_ROPE, page), _page_map(layer, n_pages, npg, i)) for i in range(npg)],
        out_specs=seq((B_HEADS, rank)),
        scratch_shapes=[pltpu.VMEM((npg * page, rank), BF16), pltpu.VMEM((B_ROPE, npg * page), BF16),
                        pltpu.VMEM((B_HEADS, 1), F32), pltpu.VMEM((B_HEADS, 1), F32),
                        pltpu.VMEM((B_HEADS, rank), F32)])
    return pl.pallas_call(
        functools.partial(_mla_dec_kernel, npg=npg, page=page), grid_spec=grid_spec,
        out_shape=jax.ShapeDtypeStruct((ns, B_HEADS, rank), F32),
        compiler_params=_cp("arbitrary", "arbitrary"), name="mla_dec")(
            pt_flat, ql, qr, cn, kn, *([lat_pool] * npg), *([krt_pool] * npg))


def _fox_dec_kernel(pt_ref, q_ref, kn_ref, vn_ref, lfn_ref, *rest, npg, page):
    k_refs, v_refs, lf_refs = rest[:npg], rest[npg:2 * npg], rest[2 * npg:3 * npg]
    o_ref, k_s, v_s, m_s, l_s, acc_s, carry_s = rest[3 * npg:]
    c = pl.program_id(1)
    nh = D_HEADS
    width = nh * D_HEAD_DIM

    @pl.when(c == 0)
    def _init():
        m_s[...] = jnp.full_like(m_s, NEG)
        l_s[...] = jnp.zeros_like(l_s)
        acc_s[...] = jnp.zeros_like(acc_s)
        carry_s[...] = jnp.zeros_like(carry_s)

    for i in range(npg):
        k_s[:, i * page:(i + 1) * page] = k_refs[i][0, 0].astype(BF16)
        v_s[:, i * page:(i + 1) * page] = v_refs[i][0, 0].astype(BF16)

    diag = (lax.broadcasted_iota(jnp.int32, (nh, width), 1) // D_HEAD_DIM
            == lax.broadcasted_iota(jnp.int32, (nh, width), 0))
    qbd = jnp.where(diag, q_ref[0].astype(F32), 0.0)

    rows = nh * npg
    y = jnp.concatenate([lf_refs[i][0, 0] for i in range(npg)], axis=0)
    upper = (lax.broadcasted_iota(jnp.int32, (page, page), 0) <= lax.broadcasted_iota(jnp.int32, (page, page), 1))
    upper = jnp.where(upper, 1.0, 0.0).astype(BF16)
    z = sum(_dot(piece, upper) for piece in _split3(y))
    r0 = lax.broadcasted_iota(jnp.int32, (rows, rows), 0)
    r1 = lax.broadcasted_iota(jnp.int32, (rows, rows), 1)
    earlier = jnp.where(((r0 % nh) == (r1 % nh)) & (r1 // nh < r0 // nh), 1.0, 0.0).astype(BF16)
    totals = jnp.broadcast_to(z[:, page - 1:page], (rows, page))
    off = sum(_dot(earlier, piece) for piece in _split3(totals))
    cum = z + off + jnp.concatenate([carry_s[...]] * npg, axis=0)
    carry_s[...] = cum[rows - nh:rows, page - 1:page]
    bias = jnp.concatenate([cum[nh * i:nh * (i + 1), :] for i in range(npg)], axis=1)

    s = _dot(qbd.astype(BF16), k_s[...]) - bias
    _online_update(s, m_s, l_s, acc_s, v_s[...], True)

    @pl.when(c == pl.num_programs(1) - 1)
    def _fin():
        kn = kn_ref[0].astype(BF16).astype(F32)
        vn = vn_ref[0].astype(BF16).astype(F32)
        s_new = jnp.sum(qbd * kn, -1, keepdims=True) - (carry_s[...] + lfn_ref[0])
        m_prev = m_s[...]
        m_new = jnp.maximum(m_prev, s_new)
        alpha = jnp.exp(m_prev - m_new)
        p_new = jnp.exp(s_new - m_new)
        o = (alpha * acc_s[...] + p_new * vn) / (alpha * l_s[...] + p_new)
        o_ref[0] = jnp.sum(jnp.where(diag, o, 0.0), axis=0, keepdims=True)


def _fox_dec(pt_flat, q, kn, vn, lfn, kt_pool, vt_pool, lft_pool, layer, n_pages, npg):
    ns = q.shape[0]
    width, page = kt_pool.shape[2], kt_pool.shape[3]
    seq = lambda shape: pl.BlockSpec((1,) + shape, lambda b, c, pt: (b, 0, 0))
    pages = lambda shape: [pl.BlockSpec((1, 1) + shape, _page_map(layer, n_pages, npg, i)) for i in range(npg)]
    grid_spec = pltpu.PrefetchScalarGridSpec(
        num_scalar_prefetch=1, grid=(ns, n_pages // npg),
        in_specs=[seq((1, width)), seq((1, width)), seq((1, width)), seq((D_HEADS, 1))]
        + pages((width, page)) + pages((width, page)) + pages((D_HEADS, page)),
        out_specs=seq((1, width)),
        scratch_shapes=[pltpu.VMEM((width, npg * page), BF16), pltpu.VMEM((width, npg * page), BF16),
                        pltpu.VMEM((D_HEADS, 1), F32), pltpu.VMEM((D_HEADS, 1), F32),
                        pltpu.VMEM((D_HEADS, width), F32), pltpu.VMEM((D_HEADS, 1), F32)])
    return pl.pallas_call(
        functools.partial(_fox_dec_kernel, npg=npg, page=page), grid_spec=grid_spec,
        out_shape=jax.ShapeDtypeStruct((ns, 1, width), F32),
        compiler_params=_cp("arbitrary", "arbitrary"), name="fox_dec")(
            pt_flat, q, kn, vn, lfn, *([kt_pool] * npg), *([vt_pool] * npg), *([lft_pool] * npg))


def _rope_tables(pos):
    half = B_ROPE // 2
    inv = ROPE_THETA ** (-jnp.arange(half, dtype=F32) / half)
    ang = pos.astype(F32)[:, None] * inv
    cos = jnp.concatenate([jnp.cos(ang)] * 2, axis=-1)
    sin = jnp.concatenate([jnp.sin(ang)] * 2, axis=-1)
    return cos, sin, jnp.tile(cos, (1, B_HEADS)), jnp.tile(sin, (1, B_HEADS))


def _rot_cols(w):
    half = w.shape[-1] // 2
    return jnp.concatenate([-w[..., half:], w[..., :half]], axis=-1)


def _prep_even(w_in, w_q_b, w_kv_b):
    d = w_in.shape[0]
    kr = w_in[:, E_KR:E_KRR]
    w_in_p = jnp.concatenate([w_in, _rot_cols(kr), jnp.zeros((d, E_COLS - E_END), w_in.dtype)], axis=1)
    qr = w_q_b.shape[0]
    wq = w_q_b.reshape(qr, B_HEADS, B_NOPE + B_ROPE)
    rp = wq[:, :, B_NOPE:]
    wq_p = jnp.concatenate([wq[:, :, :B_NOPE].reshape(qr, -1), rp.reshape(qr, -1), _rot_cols(rp).reshape(qr, -1)],
                           axis=1)
    wkn = jnp.transpose(w_kv_b[:, :, :B_NOPE], (1, 2, 0))
    wkv = jnp.transpose(w_kv_b[:, :, B_NOPE:], (1, 0, 2))
    return w_in_p.astype(BF16), wq_p.astype(BF16), wkn.astype(BF16), wkv.astype(BF16)


def kernel(x_prompt, x_sample, cache_mla_latent, cache_mla_krope, state_conv_a, cache_fox_k, cache_fox_v,
           cache_fox_logf, state_conv_c, page_table, c_prompt, c_sample, w_ada, b_ada, ln_g, ln_b,
           w_in_even, w_q_b, w_kv_b, q_norm, kv_norm, conv_a, w_out_even, w_in_odd, b_f, conv_c,
           conv_c_b, cn_g, cn_b, w_out_odd, w_ff1, w_ff2):
    depth = w_ada.shape[0]
    alpha = float((2 * depth) ** 0.25)
    nbp, seq, d = x_prompt.shape
    ns = x_sample.shape[0]
    n_pages = page_table.shape[1]
    page = cache_mla_latent.shape[2]
    past_len = n_pages * page
    n_pool = cache_fox_k.shape[1]
    aw = conv_a.shape[-1]
    cw = conv_c.shape[-1]
    dw = D_HEADS * D_HEAD_DIM

    tm = min(256, seq)
    tq_mla, tk_mla = min(256, seq), min(1024, seq)
    tq_fox, tk_fox = min(512, seq), min(512, seq)
    tm_ffn, tf = min(1024, seq), min(1024, w_ff1.shape[-1])
    npg = min(16, n_pages)

    n_c = nbp + ns
    n_pad = -n_c % 8
    c_all = jnp.concatenate([c_prompt, c_sample, jnp.zeros((n_pad, d), F32)], axis=0)
    mods = _ada(c_all, w_ada, b_ada)
    mod_p = jnp.transpose(mods[:, :, :nbp], (0, 2, 1, 3))[:, :, :, None, :]
    mod_s = mods[:, None, :, nbp:n_c]

    tabs_p = _rope_tables(jnp.arange(seq, dtype=jnp.int32))
    tabs_s = _rope_tables(jnp.full((ns,), past_len, dtype=jnp.int32))
    pt_flat = page_table.reshape(-1)
    keys_minor = lambda a: jnp.transpose(a, (0, 1, 3, 4, 2)).reshape(a.shape[:2] + (-1, a.shape[2]))
    kt_pool = keys_minor(cache_fox_k)
    vt_pool = keys_minor(cache_fox_v)
    lft_pool = jnp.swapaxes(cache_fox_logf, 2, 3)
    krt_pool = jnp.swapaxes(cache_mla_krope, 2, 3)

    xp = x_prompt
    xs = x_sample.reshape(1, ns, d)
    even_p, even_s, odd_p, odd_s = [], [], [], []
    for l in range(depth):
        i = l // 2
        row = lambda a: a.reshape(1, -1)
        g0, b0, g1, b1 = row(ln_g[l, 0]), row(ln_b[l, 0]), row(ln_g[l, 1]), row(ln_b[l, 1])
        if l % 2 == 0:
            w_in_p, wq_p, wkn, wkv = _prep_even(w_in_even[i], w_q_b[i], w_kv_b[i])
            w_out = w_out_even[i].astype(BF16)
            tail = (conv_a[i], w_out[:aw], w_out[aw:], g0, b0)
            qn, kvn = row(q_norm[i]), row(kv_norm[i])
            u, gb, ckv, kr, kcat, qnope, qrope = _even_in(xp, mod_p[l], w_in_p, qn, kvn, wq_p, *tabs_p, tm)
            yb = _mla_attn(qnope, qrope, kcat, wkn, wkv, tq_mla, tk_mla)
            xp = _mixer_out(_even_out_kernel, u, None, (gb, yb), (_row_spec(tm, aw), _row_spec(tm, yb.shape[-1])),
                            xp, mod_p[l], tail, tm, alpha, False, A_CONV)
            even_p.append((ckv, kr, u[:, seq - (A_CONV - 1):]))
            u, gb, ckv, kr, _, qnope, qrope = _even_in(xs, mod_s[l], w_in_p, qn, kvn, wq_p, *tabs_s, ns)
            ql = jnp.transpose(_qlat(qnope[0], wkn), (1, 0, 2))
            qr = qrope[0].reshape(ns, B_HEADS, B_ROPE)
            o_lat = _mla_dec(pt_flat, ql, qr, ckv[0][:, None], kr[0][:, None], cache_mla_latent, krt_pool,
                             i, n_pages, npg)
            yb = _yb(jnp.transpose(o_lat, (1, 0, 2)), wkv)[None]
            hist = jnp.transpose(state_conv_a[i], (1, 0, 2))
            xs = _mixer_out(_even_out_kernel, u, hist, (gb, yb), (_row_spec(ns, aw), _row_spec(ns, yb.shape[-1])),
                            xs, mod_s[l], tail, ns, alpha, True, A_CONV)
            even_s.append((ckv[0][:, None], kr[0][:, None],
                           jnp.concatenate([state_conv_a[i][:, 1:], u[0][:, None]], axis=1)))
        else:
            w_in_p = jnp.concatenate([w_in_odd[i], jnp.zeros((d, O_COLS - O_END), F32)], axis=1).astype(BF16)
            w_out = w_out_odd[i].astype(BF16)
            tail = (conv_c[i], row(conv_c_b[i]), row(cn_g[i]), row(cn_b[i]), w_out[:cw], w_out[cw:], g0, b0)
            bf = row(b_f[i])
            u, q, k, v, kb, vb, lf, f_hi, f_mid, f_lo = _odd_in(xp, mod_p[l], w_in_p, bf, tm, True)
            heads = lambda a: a.reshape(nbp, seq, D_HEADS, -1)
            n_zero = FOX_KC - D_HEAD_DIM - 3
            ka = jnp.concatenate([heads(kb), jnp.stack([f_hi, f_mid, f_lo], axis=-1),
                                  jnp.zeros((nbp, seq, D_HEADS, n_zero), BF16)], axis=-1)
            ka = jnp.transpose(ka, (0, 2, 1, 3))
            qa_t = jnp.concatenate([jnp.transpose(heads(q), (0, 2, 3, 1)), -jnp.ones((nbp, D_HEADS, 3, seq), BF16),
                                    jnp.zeros((nbp, D_HEADS, n_zero, seq), BF16)], axis=2)
            v_t = jnp.transpose(heads(vb), (0, 2, 3, 1))
            o_t = _fox_attn(ka, qa_t, v_t, tq_fox, tk_fox).reshape(nbp, dw, seq)
            xp = _mixer_out(_odd_out_kernel, u, None, (o_t,), (_time_minor_spec(dw, tm),),
                            xp, mod_p[l], tail, tm, alpha, False, C_CONV)
            hd = (nbp, seq, D_HEADS, D_HEAD_DIM)
            odd_p.append((k.reshape(hd), v.reshape(hd), lf, u[:, seq - (C_CONV - 1):]))
            u, q, k, v, kb, vb, lf = _odd_in(xs, mod_s[l], w_in_p, bf, ns, False)
            o = _fox_dec(pt_flat, q[0][:, None], k[0][:, None], v[0][:, None], lf[0][:, :, None],
                         kt_pool, vt_pool, lft_pool, i, n_pages, npg)
            o_t = jnp.transpose(o.reshape(ns, dw)).astype(BF16)[None]
            hist = jnp.transpose(state_conv_c[i], (1, 0, 2))
            xs = _mixer_out(_odd_out_kernel, u, hist, (o_t,), (_time_minor_spec(dw, ns),),
                            xs, mod_s[l], tail, ns, alpha, True, C_CONV)
            hd = (ns, 1, D_HEADS, D_HEAD_DIM)
            odd_s.append((k[0].reshape(hd), v[0].reshape(hd), lf[0][:, None],
                          jnp.concatenate([state_conv_c[i][:, 1:], u[0][:, None]], axis=1)))
        w1, w2 = w_ff1[l].astype(BF16), w_ff2[l].astype(BF16)
        xp = _ffn(xp, mod_p[l], w1, w2, g1, b1, tm_ffn, tf, alpha)
        xs = _ffn(xs, mod_s[l], w1, w2, g1, b1, ns, tf, alpha)

    stack = lambda states: [jnp.stack(s) for s in zip(*states)]
    lat_p, kr_p, ca_p = stack(even_p)
    k_p, v_p, lf_p, cc_p = stack(odd_p)
    lat_s, kr_s, ca_s = stack(even_s)
    k_s, v_s, lf_s, cc_s = stack(odd_s)
    return (xp, xs.reshape(ns, 1, d), lat_p, kr_p, ca_p, k_p, v_p, lf_p, cc_p,
            lat_s, kr_s, ca_s, k_s, v_s, lf_s, cc_s)
```

```python
import functools

import numpy as np
import jax
import jax.numpy as jnp
from jax import lax
from jax.experimental import pallas as pl
from jax.experimental.pallas import tpu as pltpu

F32 = jnp.float32
BF16 = jnp.bfloat16

A_CONV = 3
C_CONV = 31
B_HEADS = 8
B_NOPE = 64
B_ROPE = 32
B_V = 64
D_HEADS = 8
D_HEAD_DIM = 64
ROPE_THETA = 10000.0
MLA_SCALE = (B_NOPE + B_ROPE) ** -0.5
FOX_SCALE = D_HEAD_DIM ** -0.5
LN_EPS = 1e-5
RMS_EPS = 1e-6
NEG = -1e30
LANE = 128
SUBLANE = 8
VMEM_LIMIT = 56 * 1024 * 1024

_NT = (((1,), (1,)), ((), ()))


def _cp(*sem):
    return pltpu.CompilerParams(dimension_semantics=sem, vmem_limit_bytes=VMEM_LIMIT)


def _ln(y, g, b):
    mu = jnp.mean(y, -1, keepdims=True)
    d = y - mu
    var = jnp.mean(d * d, -1, keepdims=True)
    return d * lax.rsqrt(var + LN_EPS) * g + b


def _rms(y, g):
    return y * lax.rsqrt(jnp.mean(y * y, -1, keepdims=True) + RMS_EPS) * g


def _split3(x):
    hi = x.astype(BF16)
    r = x - hi.astype(F32)
    mid = r.astype(BF16)
    lo = (r - mid.astype(F32)).astype(BF16)
    return hi, mid, lo


def _dot(a, b):
    return jnp.dot(a, b, preferred_element_type=F32)


def _dot_nt(a, b):
    return lax.dot_general(a, b, _NT, preferred_element_type=F32)


def _mod_spec(r, d):
    if r == 1:
        return pl.BlockSpec((1, 6, 1, d), lambda b, i, *_: (b, 0, 0, 0))
    return pl.BlockSpec((1, 6, r, d), lambda b, i, *_: (b, 0, i, 0))


def _row_spec(tm, c):
    return pl.BlockSpec((1, tm, c), lambda b, i, *_: (b, i, 0))


def _time_minor_spec(c, tm):
    return pl.BlockSpec((1, c, tm), lambda b, i, *_: (b, 0, i))


def _full_spec(shape):
    return pl.BlockSpec(shape, lambda *_: (0,) * len(shape))


def _ada_kernel(c_ref, w_ref, b_ref, o_ref):
    c = c_ref[...]
    s = (c * jax.nn.sigmoid(c)).astype(BF16)
    o_ref[0, 0] = _dot(s, w_ref[0].astype(BF16)) + b_ref[0]


def _ada(c_all, w_ada, b_ada):
    depth, d, d6 = w_ada.shape
    n = c_all.shape[0]
    return pl.pallas_call(
        _ada_kernel, grid=(depth, d6 // d),
        in_specs=[pl.BlockSpec((n, d), lambda l, j: (0, 0)),
                  pl.BlockSpec((1, d, d), lambda l, j: (l, 0, j)),
                  pl.BlockSpec((1, 1, d), lambda l, j: (l, 0, j))],
        out_specs=pl.BlockSpec((1, 1, n, d), lambda l, j: (l, j, 0, 0)),
        out_shape=jax.ShapeDtypeStruct((depth, d6 // d, n, d), F32),
        compiler_params=_cp("arbitrary", "arbitrary"), name="ada")(c_all, w_ada, b_ada.reshape(depth, 1, d6))


E_XA, E_GB, E_GC, E_QA, E_KV, E_KR, E_KRR, E_END = 0, 512, 1024, 1536, 1920, 2176, 2208, 2240
E_COLS = 2304
Q_NOPE, Q_ROPE, Q_ROT, Q_END = 0, 512, 768, 1024
MLA_RANK = E_KR - E_KV
MLA_KC = 384


def _even_in_kernel(x_ref, mod_ref, w_ref, qn_ref, kvn_ref, wq_ref, ck_ref, sk_ref, cq_ref, sq_ref,
                    u_ref, gb_ref, ckv_ref, kr_ref, kcat_ref, qnope_ref, qrope_ref):
    m = mod_ref[0]
    h = x_ref[0] * (1.0 + m[1]) + m[0]
    z = _dot(h.astype(BF16), w_ref[...])
    u_ref[0] = z[:, E_GC:E_QA] * z[:, E_XA:E_GB]
    gb_ref[0] = z[:, E_GB:E_GC]
    ckv = _rms(z[:, E_KV:E_KR], kvn_ref[...])
    ckv_ref[0] = ckv
    kr = z[:, E_KR:E_KRR] * ck_ref[...] + z[:, E_KRR:E_END] * sk_ref[...]
    kr_ref[0] = kr
    kcat_ref[0, :, 0:MLA_RANK] = ckv.astype(BF16)
    kcat_ref[0, :, MLA_RANK:MLA_RANK + B_ROPE] = kr.astype(BF16)
    kcat_ref[0, :, MLA_RANK + B_ROPE:] = jnp.zeros((kr.shape[0], MLA_KC - MLA_RANK - B_ROPE), BF16)
    qn = _rms(z[:, E_QA:E_KV], qn_ref[...])
    q = _dot(qn.astype(BF16), wq_ref[...])
    qnope_ref[0] = q[:, Q_NOPE:Q_ROPE].astype(BF16)
    qr = q[:, Q_ROPE:Q_ROT] * cq_ref[...] + q[:, Q_ROT:Q_END] * sq_ref[...]
    qrope_ref[0] = (qr * MLA_SCALE).astype(BF16)


def _even_in(x, mod, w_in, qn, kvn, wq, ck, sk, cq, sq, tm):
    nb, t, d = x.shape
    r = mod.shape[2]
    aw = E_GB - E_XA
    outs = [(aw, F32), (aw, F32), (MLA_RANK, F32), (B_ROPE, F32), (MLA_KC, BF16),
            (B_HEADS * B_NOPE, BF16), (B_HEADS * B_ROPE, BF16)]
    tab = lambda c: pl.BlockSpec((tm, c), lambda b, i: (i, 0))
    return pl.pallas_call(
        _even_in_kernel, grid=(nb, t // tm),
        in_specs=[_row_spec(tm, d), _mod_spec(r, d), _full_spec(w_in.shape), _full_spec(qn.shape),
                  _full_spec(kvn.shape), _full_spec(wq.shape), tab(B_ROPE), tab(B_ROPE),
                  tab(B_HEADS * B_ROPE), tab(B_HEADS * B_ROPE)],
        out_specs=[_row_spec(tm, c) for c, _ in outs],
        out_shape=[jax.ShapeDtypeStruct((nb, t, c), dt) for c, dt in outs],
        compiler_params=_cp("arbitrary", "arbitrary"), name="even_in")(x, mod, w_in, qn, kvn, wq, ck, sk, cq, sq)


def _causal_pairs(t, tq, tk):
    qi, kj, last = [], [], []
    for i in range(t // tq):
        n = ((i + 1) * tq - 1) // tk + 1
        for j in range(n):
            qi.append(i)
            kj.append(j)
            last.append(int(j == n - 1))
    return (np.asarray(qi, np.int32), np.asarray(kj, np.int32), np.asarray(last, np.int32))


ATTN_LOOKAHEAD = 2


def _mla_attn_kernel(qi_ref, kj_ref, last_ref, qn_ref, qr_ref, kc_ref, wkn_ref, wkv_ref, yb_ref,
                     qc_s, m_s, l_s, acc_s, *, tq, tk):
    step = pl.program_id(1)
    qi = qi_ref[step]
    kj = kj_ref[step]

    @pl.when(kj == 0)
    def _init():
        qn = qn_ref[0]
        qr = qr_ref[0]
        for h in range(B_HEADS):
            ql = _dot(qn[:, h * B_NOPE:(h + 1) * B_NOPE], wkn_ref[h]) * MLA_SCALE
            qc_s[h * tq:(h + 1) * tq, 0:MLA_RANK] = ql.astype(BF16)
            qc_s[h * tq:(h + 1) * tq, MLA_RANK:MLA_RANK + B_ROPE] = qr[:, h * B_ROPE:(h + 1) * B_ROPE]
            qc_s[h * tq:(h + 1) * tq, MLA_RANK + B_ROPE:] = jnp.zeros((tq, MLA_KC - MLA_RANK - B_ROPE), BF16)
        m_s[...] = jnp.full_like(m_s, NEG)
        l_s[...] = jnp.zeros_like(l_s)
        acc_s[...] = jnp.zeros_like(acc_s)

    def update(masked):
        kc = kc_ref[0]
        v = kc[:, :MLA_RANK]

        def scores(h):
            return _dot_nt(qc_s[h * tq:(h + 1) * tq, :], kc)

        pending = [scores(h) for h in range(ATTN_LOOKAHEAD)]
        for h in range(B_HEADS):
            rows = slice(h * tq, (h + 1) * tq)
            s = pending.pop(0)
            if h + ATTN_LOOKAHEAD < B_HEADS:
                pending.append(scores(h + ATTN_LOOKAHEAD))
            if masked:
                ri = lax.broadcasted_iota(jnp.int32, s.shape, 0)
                ci = lax.broadcasted_iota(jnp.int32, s.shape, 1)
                s = jnp.where(ri - ci >= kj * tk - qi * tq, s, NEG)
            m_prev = m_s[rows, :]
            m_new = jnp.maximum(m_prev, jnp.max(s, -1, keepdims=True))
            alpha = jnp.exp(m_prev - m_new)
            p = jnp.exp(s - m_new)
            l_s[rows, :] = alpha * l_s[rows, :] + jnp.sum(p, -1, keepdims=True)
            acc_s[rows, :] = alpha * acc_s[rows, :] + _dot(p.astype(BF16), v)
            m_s[rows, :] = m_new

    needs_mask = (kj + 1) * tk - 1 > qi * tq

    @pl.when(needs_mask)
    def _masked():
        update(True)

    @pl.when(jnp.logical_not(needs_mask))
    def _plain():
        update(False)

    @pl.when(last_ref[step] == 1)
    def _fin():
        o = acc_s[...] / l_s[...]
        ys = [_dot(o[h * tq:(h + 1) * tq].astype(BF16), wkv_ref[h]) for h in range(B_HEADS)]
        yb_ref[0] = jnp.concatenate(ys, axis=1).astype(yb_ref.dtype)


def _mla_attn(qnope, qrope, kcat, wkn, wkv, tq, tk):
    nb, t, _ = qnope.shape
    qi, kj, last = _causal_pairs(t, tq, tk)
    grid_spec = pltpu.PrefetchScalarGridSpec(
        num_scalar_prefetch=3, grid=(nb, len(qi)),
        in_specs=[pl.BlockSpec((1, tq, qnope.shape[-1]), lambda b, s, qi, kj, la: (b, qi[s], 0)),
                  pl.BlockSpec((1, tq, qrope.shape[-1]), lambda b, s, qi, kj, la: (b, qi[s], 0)),
                  pl.BlockSpec((1, tk, MLA_KC), lambda b, s, qi, kj, la: (b, kj[s], 0)),
                  _full_spec(wkn.shape), _full_spec(wkv.shape)],
        out_specs=pl.BlockSpec((1, tq, B_HEADS * B_V), lambda b, s, qi, kj, la: (b, qi[s], 0)),
        scratch_shapes=[pltpu.VMEM((B_HEADS * tq, MLA_KC), BF16),
                        pltpu.VMEM((B_HEADS * tq, 1), F32), pltpu.VMEM((B_HEADS * tq, 1), F32),
                        pltpu.VMEM((B_HEADS * tq, MLA_RANK), F32)])
    return pl.pallas_call(
        functools.partial(_mla_attn_kernel, tq=tq, tk=tk), grid_spec=grid_spec,
        out_shape=jax.ShapeDtypeStruct((nb, t, B_HEADS * B_V), BF16),
        compiler_params=_cp("arbitrary", "arbitrary"), name="mla_attn")(
            jnp.asarray(qi), jnp.asarray(kj), jnp.asarray(last), qnope, qrope, kcat, wkn, wkv)


CONV_ROWS = 32


def _conv_prompt(u, halo_ref, ext_s, w_ref, i, k):
    tm = u.shape[0]
    hr = halo_ref.shape[1]
    ext_s[0, 0:hr] = jnp.where(i > 0, halo_ref[0], 0.0)
    ext_s[0, hr:hr + tm] = u
    n = hr + tm - SUBLANE
    for p in range(1, SUBLANE):
        ext_s[p, 0:n] = ext_s[0, p:p + n]
    off = hr - (k - 1)
    blocks = []
    for r in range(0, tm, CONV_ROWS):
        acc = w_ref[k - 1:k] * ext_s[0, hr + r:hr + r + CONV_ROWS]
        for j in range(k - 1):
            p = (off + j) % SUBLANE
            base = off + j + r - p
            acc = acc + w_ref[j:j + 1] * ext_s[p, base:base + CONV_ROWS]
        blocks.append(acc)
    return jnp.concatenate(blocks, axis=0)


def _conv_sample(u, hist_ref, w_ref, k):
    acc = w_ref[k - 1:k] * u
    for j in range(k - 1):
        acc = acc + w_ref[j:j + 1] * hist_ref[j]
    return acc


def _halo_spec(tm, hr, c):
    return pl.BlockSpec((1, hr, c), lambda b, i: (b, jnp.maximum(i * (tm // hr) - 1, 0), 0))


def _even_out_kernel(*refs, alpha, sample):
    if sample:
        (u_ref, hist_ref, gb_ref, yb_ref, x_ref, mod_ref, cw_ref, wa_ref, wb_ref, g_ref, b_ref, o_ref) = refs
        conv = _conv_sample(u_ref[0], hist_ref, cw_ref, A_CONV)
    else:
        (u_ref, halo_ref, gb_ref, yb_ref, x_ref, mod_ref, cw_ref, wa_ref, wb_ref, g_ref, b_ref, o_ref, ext_s) = refs
        conv = _conv_prompt(u_ref[0], halo_ref, ext_s, cw_ref, pl.program_id(1), A_CONV)
    ya = gb_ref[0] * conv
    mix = _dot(ya.astype(BF16), wa_ref[...]) + _dot(yb_ref[0], wb_ref[...])
    m = mod_ref[0]
    o_ref[0] = _ln(alpha * x_ref[0] + (1.0 + m[2]) * mix, g_ref[...], b_ref[...])


def _mixer_out(kernel, u, side, others, other_specs, x, mod, tail, tm, alpha, sample, k):
    nb, t, d = x.shape
    c = u.shape[-1]
    r = mod.shape[2]
    hr = 8 if k - 1 <= 8 else 32
    if sample:
        side_arg, side_spec, scratch = side, _full_spec(side.shape), []
    else:
        side_arg, side_spec, scratch = u, _halo_spec(tm, hr, c), [pltpu.VMEM((SUBLANE, tm + hr, c), F32)]
    return pl.pallas_call(
        functools.partial(kernel, alpha=alpha, sample=sample), grid=(nb, t // tm),
        in_specs=[_row_spec(tm, c), side_spec] + list(other_specs)
        + [_row_spec(tm, d), _mod_spec(r, d)] + [_full_spec(w.shape) for w in tail],
        out_specs=_row_spec(tm, d), out_shape=jax.ShapeDtypeStruct((nb, t, d), F32),
        scratch_shapes=scratch, compiler_params=_cp("arbitrary", "arbitrary"),
        name=kernel.__name__.strip("_"))(u, side_arg, *others, x, mod, *tail)


def _ffn_kernel(x_ref, mod_ref, w1_ref, w2_ref, g_ref, b_ref, o_ref, h_s, acc_s, *, alpha):
    f = pl.program_id(2)

    @pl.when(f == 0)
    def _first():
        m = mod_ref[0]
        h_s[...] = (x_ref[0] * (1.0 + m[4]) + m[3]).astype(BF16)
        acc_s[...] = jnp.zeros_like(acc_s)

    a = jnp.square(jnp.maximum(_dot(h_s[...], w1_ref[...]), 0.0))
    acc_s[...] += _dot(a.astype(BF16), w2_ref[...])

    @pl.when(f == pl.num_programs(2) - 1)
    def _last():
        m = mod_ref[0]
        o_ref[0] = _ln(alpha * x_ref[0] + (1.0 + m[5]) * acc_s[...], g_ref[...], b_ref[...])


def _ffn(x, mod, w1, w2, g, b, tm, tf, alpha):
    nb, t, d = x.shape
    r = mod.shape[2]
    ff = w1.shape[1]
    return pl.pallas_call(
        functools.partial(_ffn_kernel, alpha=alpha), grid=(nb, t // tm, ff // tf),
        in_specs=[_row_spec(tm, d), _mod_spec(r, d),
                  pl.BlockSpec((d, tf), lambda b, i, f: (0, f)), pl.BlockSpec((tf, d), lambda b, i, f: (f, 0)),
                  _full_spec(g.shape), _full_spec(b.shape)],
        out_specs=_row_spec(tm, d), out_shape=jax.ShapeDtypeStruct((nb, t, d), F32),
        scratch_shapes=[pltpu.VMEM((tm, d), BF16), pltpu.VMEM((tm, d), F32)],
        compiler_params=_cp("arbitrary", "arbitrary", "arbitrary"), name="ffn")(x, mod, w1, w2, g, b)


O_GA, O_GB, O_Q, O_K, O_V, O_F, O_END = 0, 512, 1024, 1536, 2048, 2560, 2568
O_COLS = 2688


def _log_sigmoid(x):
    return jnp.minimum(x, 0.0) - jnp.log(1.0 + jnp.exp(-jnp.abs(x)))


FOX_KC = 128
FOX_DW = O_K - O_Q


def _fox_key_placement():
    p = np.zeros((FOX_DW + 3 * LANE, D_HEADS * FOX_KC), np.float32)
    for h in range(D_HEADS):
        for c in range(D_HEAD_DIM):
            p[h * D_HEAD_DIM + c, h * FOX_KC + c] = 1.0
        for piece in range(3):
            p[FOX_DW + piece * LANE + h, h * FOX_KC + D_HEAD_DIM + piece] = 1.0
    return jnp.asarray(p, BF16)


def _odd_in_kernel(*refs, prompt):
    if prompt:
        (x_ref, mod_ref, w_ref, bf_ref, place_ref, u_ref, kt_ref, vt_ref, lf_ref, ka_ref, qt_ref, vtb_ref,
         carry_s) = refs
    else:
        (x_ref, mod_ref, w_ref, bf_ref, u_ref, q_ref, k_ref, v_ref, lf_ref) = refs
    m = mod_ref[0]
    h = x_ref[0] * (1.0 + m[1]) + m[0]
    z = _dot(h.astype(BF16), w_ref[...])
    tm = z.shape[0]
    u_ref[0] = z[:, O_GA:O_GB] * jax.nn.sigmoid(z[:, O_GB:O_Q])
    q = z[:, O_Q:O_K] * FOX_SCALE
    k = z[:, O_K:O_V]
    v = z[:, O_V:O_F]
    lane = lax.broadcasted_iota(jnp.int32, (tm, O_COLS - O_F), 1)
    lf = jnp.where(lane < D_HEADS, _log_sigmoid(z[:, O_F:O_COLS] + bf_ref[...]), 0.0)
    lf_ref[0] = lf[:, :D_HEADS]
    if not prompt:
        q_ref[0] = q.astype(BF16)
        k_ref[0] = k
        v_ref[0] = v
        return

    kt_ref[0] = k.T
    vt = v.T
    vt_ref[0] = vt
    vtb_ref[0] = vt.astype(BF16)
    qt_ref[0] = q.T.astype(BF16)

    @pl.when(pl.program_id(1) == 0)
    def _reset():
        carry_s[...] = jnp.zeros_like(carry_s)

    tri = (lax.broadcasted_iota(jnp.int32, (tm, tm), 0) >= lax.broadcasted_iota(jnp.int32, (tm, tm), 1))
    tri = jnp.where(tri, 1.0, 0.0).astype(BF16)
    cum = sum(_dot(tri, piece) for piece in _split3(lf)) + carry_s[...]
    carry_s[...] = cum[tm - 1:tm]
    cat = jnp.concatenate((k.astype(BF16),) + _split3(cum), axis=1)
    ka_ref[0] = _dot(cat, place_ref[...]).astype(BF16)


def _odd_in(x, mod, w_in, bf, tm, prompt):
    nb, t, d = x.shape
    r = mod.shape[2]
    cw = O_GB - O_GA
    ins = [x, mod, w_in, bf]
    in_specs = [_row_spec(tm, d), _mod_spec(r, d), _full_spec(w_in.shape), _full_spec(bf.shape)]
    if prompt:
        place = _fox_key_placement()
        ins.append(place)
        in_specs.append(_full_spec(place.shape))
        rows = [(cw, F32)]
        cols = [(FOX_DW, F32), (FOX_DW, F32)]
        rows2 = [(D_HEADS, F32), (D_HEADS * FOX_KC, BF16)]
        cols2 = [(FOX_DW, BF16), (FOX_DW, BF16)]
        out_specs = ([_row_spec(tm, c) for c, _ in rows] + [_time_minor_spec(c, tm) for c, _ in cols]
                     + [_row_spec(tm, c) for c, _ in rows2] + [_time_minor_spec(c, tm) for c, _ in cols2])
        out_shape = ([jax.ShapeDtypeStruct((nb, t, c), dt) for c, dt in rows]
                     + [jax.ShapeDtypeStruct((nb, c, t), dt) for c, dt in cols]
                     + [jax.ShapeDtypeStruct((nb, t, c), dt) for c, dt in rows2]
                     + [jax.ShapeDtypeStruct((nb, c, t), dt) for c, dt in cols2])
        scratch = [pltpu.VMEM((1, O_COLS - O_F), F32)]
    else:
        outs = [(cw, F32), (FOX_DW, BF16), (FOX_DW, F32), (FOX_DW, F32), (D_HEADS, F32)]
        out_specs = [_row_spec(tm, c) for c, _ in outs]
        out_shape = [jax.ShapeDtypeStruct((nb, t, c), dt) for c, dt in outs]
        scratch = []
    return pl.pallas_call(
        functools.partial(_odd_in_kernel, prompt=prompt), grid=(nb, t // tm),
        in_specs=in_specs, out_specs=out_specs, out_shape=out_shape,
        scratch_shapes=scratch, compiler_params=_cp("arbitrary", "arbitrary"), name="odd_in")(*ins)


def _fox_attn_kernel(qi_ref, kj_ref, last_ref, ka_ref, qt_ref, vt_ref, o_ref, qa_s, m_s, l_s, acc_s, *, tq, tk):
    step = pl.program_id(1)
    qi = qi_ref[step]
    kj = kj_ref[step]
    hd = D_HEAD_DIM

    @pl.when(kj == 0)
    def _init():
        minus_ones = jnp.where(lax.broadcasted_iota(jnp.int32, (FOX_KC - hd, tq), 0) < 3, -1.0, 0.0).astype(BF16)
        for h in range(D_HEADS):
            qa_s[h, 0:hd, :] = qt_ref[0, h * hd:(h + 1) * hd, :]
            qa_s[h, hd:FOX_KC, :] = minus_ones
        m_s[...] = jnp.full_like(m_s, NEG)
        l_s[...] = jnp.zeros_like(l_s)
        acc_s[...] = jnp.zeros_like(acc_s)

    def update(masked):
        def scores(h):
            return _dot(ka_ref[0, :, h * FOX_KC:(h + 1) * FOX_KC], qa_s[h])

        pending = [scores(h) for h in range(ATTN_LOOKAHEAD)]
        for h in range(D_HEADS):
            s = pending.pop(0)
            if h + ATTN_LOOKAHEAD < D_HEADS:
                pending.append(scores(h + ATTN_LOOKAHEAD))
            if masked:
                ki = lax.broadcasted_iota(jnp.int32, s.shape, 0)
                qq = lax.broadcasted_iota(jnp.int32, s.shape, 1)
                s = jnp.where(qq - ki >= kj * tk - qi * tq, s, NEG)
            m_prev = m_s[h]
            m_new = jnp.maximum(m_prev, jnp.max(s, 0, keepdims=True))
            alpha = jnp.exp(m_prev - m_new)
            p = jnp.exp(s - m_new)
            l_s[h] = alpha * l_s[h] + jnp.sum(p, 0, keepdims=True)
            acc_s[h] = alpha * acc_s[h] + _dot(vt_ref[0, h * hd:(h + 1) * hd, :], p.astype(BF16))
            m_s[h] = m_new

    needs_mask = (kj + 1) * tk - 1 > qi * tq

    @pl.when(needs_mask)
    def _masked():
        update(True)

    @pl.when(jnp.logical_not(needs_mask))
    def _plain():
        update(False)

    @pl.when(last_ref[step] == 1)
    def _fin():
        for h in range(D_HEADS):
            o_ref[0, h * hd:(h + 1) * hd, :] = (acc_s[h] / l_s[h]).astype(o_ref.dtype)


def _fox_attn(ka, q_t, v_t, tq, tk):
    nb, t, _ = ka.shape
    dw = v_t.shape[1]
    qi, kj, last = _causal_pairs(t, tq, tk)
    qmap = lambda b, s, qi, kj, la: (b, 0, qi[s])
    grid_spec = pltpu.PrefetchScalarGridSpec(
        num_scalar_prefetch=3, grid=(nb, len(qi)),
        in_specs=[pl.BlockSpec((1, tk, ka.shape[-1]), lambda b, s, qi, kj, la: (b, kj[s], 0)),
                  pl.BlockSpec((1, dw, tq), qmap),
                  pl.BlockSpec((1, dw, tk), lambda b, s, qi, kj, la: (b, 0, kj[s]))],
        out_specs=pl.BlockSpec((1, dw, tq), qmap),
        scratch_shapes=[pltpu.VMEM((D_HEADS, FOX_KC, tq), BF16), pltpu.VMEM((D_HEADS, 1, tq), F32),
                        pltpu.VMEM((D_HEADS, 1, tq), F32), pltpu.VMEM((D_HEADS, D_HEAD_DIM, tq), F32)])
    return pl.pallas_call(
        functools.partial(_fox_attn_kernel, tq=tq, tk=tk), grid_spec=grid_spec,
        out_shape=jax.ShapeDtypeStruct((nb, dw, t), BF16),
        compiler_params=_cp("arbitrary", "arbitrary"), name="fox_attn")(
            jnp.asarray(qi), jnp.asarray(kj), jnp.asarray(last), ka, q_t, v_t)


def _odd_out_kernel(*refs, alpha, sample):
    if sample:
        (u_ref, hist_ref, o_in_ref, x_ref, mod_ref, cw_ref, cb_ref, cg_ref, cbb_ref, wa_ref, wb_ref, g_ref, b_ref,
         o_ref) = refs
        conv = _conv_sample(u_ref[0], hist_ref, cw_ref, C_CONV)
    else:
        (u_ref, halo_ref, o_in_ref, x_ref, mod_ref, cw_ref, cb_ref, cg_ref, cbb_ref, wa_ref, wb_ref, g_ref, b_ref,
         o_ref, ext_s) = refs
        conv = _conv_prompt(u_ref[0], halo_ref, ext_s, cw_ref, pl.program_id(1), C_CONV)
    y = _ln(conv + cb_ref[...], cg_ref[...], cbb_ref[...])
    yc = y * jax.nn.sigmoid(y)
    attn = lax.dot_general(o_in_ref[0], wb_ref[...], (((0,), (0,)), ((), ())), preferred_element_type=F32)
    mix = _dot(yc.astype(BF16), wa_ref[...]) + attn
    m = mod_ref[0]
    o_ref[0] = _ln(alpha * x_ref[0] + (1.0 + m[2]) * mix, g_ref[...], b_ref[...])


def _qlat_kernel(qn_ref, wkn_ref, o_ref):
    qn = qn_ref[...]
    for h in range(B_HEADS):
        o_ref[h] = (_dot(qn[:, h * B_NOPE:(h + 1) * B_NOPE], wkn_ref[h]) * MLA_SCALE).astype(BF16)


def _qlat(qnope, wkn):
    n = qnope.shape[0]
    return pl.pallas_call(_qlat_kernel, out_shape=jax.ShapeDtypeStruct((B_HEADS, n, wkn.shape[-1]), BF16),
                          name="qlat")(qnope, wkn)


def _yb_kernel(o_ref, wkv_ref, y_ref):
    ys = [_dot(o_ref[h].astype(BF16), wkv_ref[h]) for h in range(B_HEADS)]
    y_ref[...] = jnp.concatenate(ys, axis=1).astype(y_ref.dtype)


def _yb(o_lat_t, wkv):
    n = o_lat_t.shape[1]
    return pl.pallas_call(_yb_kernel, out_shape=jax.ShapeDtypeStruct((n, B_HEADS * B_V), BF16),
                          name="yb")(o_lat_t, wkv)


def _online_update(s, m_s, l_s, acc_s, values, values_transposed):
    m_prev = m_s[...]
    m_new = jnp.maximum(m_prev, jnp.max(s, -1, keepdims=True))
    alpha = jnp.exp(m_prev - m_new)
    p = jnp.exp(s - m_new)
    l_s[...] = alpha * l_s[...] + jnp.sum(p, -1, keepdims=True)
    pv = _dot_nt(p.astype(BF16), values) if values_transposed else _dot(p.astype(BF16), values)
    acc_s[...] = alpha * acc_s[...] + pv
    m_s[...] = m_new


def _mla_dec_kernel(pt_ref, ql_ref, qr_ref, cn_ref, kn_ref, *rest, npg, page):
    lat_refs, kr_refs = rest[:npg], rest[npg:2 * npg]
    o_ref, lat_s, kr_s, m_s, l_s, acc_s = rest[2 * npg:]
    c = pl.program_id(1)

    @pl.when(c == 0)
    def _init():
        m_s[...] = jnp.full_like(m_s, NEG)
        l_s[...] = jnp.zeros_like(l_s)
        acc_s[...] = jnp.zeros_like(acc_s)

    for i in range(npg):
        lat_s[i * page:(i + 1) * page, :] = lat_refs[i][0, 0].astype(BF16)
        kr_s[:, i * page:(i + 1) * page] = kr_refs[i][0, 0].astype(BF16)
    ql = ql_ref[0]
    qr = qr_ref[0]
    s = _dot_nt(ql, lat_s[...]) + _dot(qr, kr_s[...])
    _online_update(s, m_s, l_s, acc_s, lat_s[...], False)

    @pl.when(c == pl.num_programs(1) - 1)
    def _fin():
        cn = cn_ref[0].astype(BF16).astype(F32)
        kn = kn_ref[0].astype(BF16).astype(F32)
        s_new = (jnp.sum(ql.astype(F32) * cn, -1, keepdims=True) + jnp.sum(qr.astype(F32) * kn, -1, keepdims=True))
        m_prev = m_s[...]
        m_new = jnp.maximum(m_prev, s_new)
        alpha = jnp.exp(m_prev - m_new)
        p_new = jnp.exp(s_new - m_new)
        o_ref[0] = (alpha * acc_s[...] + p_new * cn) / (alpha * l_s[...] + p_new)


def _page_map(layer, n_pages, npg, i):
    return lambda b, c, pt: (layer, pt[b * n_pages + c * npg + i], 0, 0)


def _mla_dec(pt_flat, ql, qr, cn, kn, lat_pool, krt_pool, layer, n_pages, npg):
    ns = ql.shape[0]
    page, rank = lat_pool.shape[2], lat_pool.shape[3]
    seq = lambda shape: pl.BlockSpec((1,) + shape, lambda b, c, pt: (b, 0, 0))
    grid_spec = pltpu.PrefetchScalarGridSpec(
        num_scalar_prefetch=1, grid=(ns, n_pages // npg),
        in_specs=[seq((B_HEADS, rank)), seq((B_HEADS, B_ROPE)), seq((1, rank)), seq((1, B_ROPE))]
        + [pl.BlockSpec((1, 1, page, rank), _page_map(layer, n_pages, npg, i)) for i in range(npg)]
        + [pl.BlockSpec((1, 1, B_ROPE, page), _page_map(layer, n_pages, npg, i)) for i in range(npg)],
        out_specs=seq((B_HEADS, rank)),
        scratch_shapes=[pltpu.VMEM((npg * page, rank), BF16), pltpu.VMEM((B_ROPE, npg * page), BF16),
                        pltpu.VMEM((B_HEADS, 1), F32), pltpu.VMEM((B_HEADS, 1), F32),
                        pltpu.VMEM((B_HEADS, rank), F32)])
    return pl.pallas_call(
        functools.partial(_mla_dec_kernel, npg=npg, page=page), grid_spec=grid_spec,
        out_shape=jax.ShapeDtypeStruct((ns, B_HEADS, rank), F32),
        compiler_params=_cp("arbitrary", "arbitrary"), name="mla_dec")(
            pt_flat, ql, qr, cn, kn, *([lat_pool] * npg), *([krt_pool] * npg))


def _fox_dec_kernel(pt_ref, q_ref, kn_ref, vn_ref, lfn_ref, *rest, npg, page):
    k_refs, v_refs, lf_refs = rest[:npg], rest[npg:2 * npg], rest[2 * npg:3 * npg]
    o_ref, k_s, v_s, m_s, l_s, acc_s, carry_s = rest[3 * npg:]
    c = pl.program_id(1)
    nh = D_HEADS
    width = nh * D_HEAD_DIM

    @pl.when(c == 0)
    def _init():
        m_s[...] = jnp.full_like(m_s, NEG)
        l_s[...] = jnp.zeros_like(l_s)
        acc_s[...] = jnp.zeros_like(acc_s)
        carry_s[...] = jnp.zeros_like(carry_s)

    for i in range(npg):
        k_s[:, i * page:(i + 1) * page] = k_refs[i][0, 0].astype(BF16)
        v_s[:, i * page:(i + 1) * page] = v_refs[i][0, 0].astype(BF16)

    diag = (lax.broadcasted_iota(jnp.int32, (nh, width), 1) // D_HEAD_DIM
            == lax.broadcasted_iota(jnp.int32, (nh, width), 0))
    qbd = jnp.where(diag, q_ref[0].astype(F32), 0.0)

    rows = nh * npg
    y = jnp.concatenate([lf_refs[i][0, 0] for i in range(npg)], axis=0)
    upper = (lax.broadcasted_iota(jnp.int32, (page, page), 0) <= lax.broadcasted_iota(jnp.int32, (page, page), 1))
    upper = jnp.where(upper, 1.0, 0.0).astype(BF16)
    z = sum(_dot(piece, upper) for piece in _split3(y))
    r0 = lax.broadcasted_iota(jnp.int32, (rows, rows), 0)
    r1 = lax.broadcasted_iota(jnp.int32, (rows, rows), 1)
    earlier = jnp.where(((r0 % nh) == (r1 % nh)) & (r1 // nh < r0 // nh), 1.0, 0.0).astype(BF16)
    totals = jnp.broadcast_to(z[:, page - 1:page], (rows, page))
    off = sum(_dot(earlier, piece) for piece in _split3(totals))
    cum = z + off + jnp.concatenate([carry_s[...]] * npg, axis=0)
    carry_s[...] = cum[rows - nh:rows, page - 1:page]
    bias = jnp.concatenate([cum[nh * i:nh * (i + 1), :] for i in range(npg)], axis=1)

    s = _dot(qbd.astype(BF16), k_s[...]) - bias
    _online_update(s, m_s, l_s, acc_s, v_s[...], True)

    @pl.when(c == pl.num_programs(1) - 1)
    def _fin():
        kn = kn_ref[0].astype(BF16).astype(F32)
        vn = vn_ref[0].astype(BF16).astype(F32)
        s_new = jnp.sum(qbd * kn, -1, keepdims=True) - (carry_s[...] + lfn_ref[0])
        m_prev = m_s[...]
        m_new = jnp.maximum(m_prev, s_new)
        alpha = jnp.exp(m_prev - m_new)
        p_new = jnp.exp(s_new - m_new)
        o = (alpha * acc_s[...] + p_new * vn) / (alpha * l_s[...] + p_new)
        o_ref[0] = jnp.sum(jnp.where(diag, o, 0.0), axis=0, keepdims=True)


def _fox_dec(pt_flat, q, kn, vn, lfn, kt_pool, vt_pool, lft_pool, layer, n_pages, npg):
    ns = q.shape[0]
    width, page = kt_pool.shape[2], kt_pool.shape[3]
    seq = lambda shape: pl.BlockSpec((1,) + shape, lambda b, c, pt: (b, 0, 0))
    pages = lambda shape: [pl.BlockSpec((1, 1) + shape, _page_map(layer, n_pages, npg, i)) for i in range(npg)]
    grid_spec = pltpu.PrefetchScalarGridSpec(
        num_scalar_prefetch=1, grid=(ns, n_pages // npg),
        in_specs=[seq((1, width)), seq((1, width)), seq((1, width)), seq((D_HEADS, 1))]
        + pages((width, page)) + pages((width, page)) + pages((D_HEADS, page)),
        out_specs=seq((1, width)),
        scratch_shapes=[pltpu.VMEM((width, npg * page), BF16), pltpu.VMEM((width, npg * page), BF16),
                        pltpu.VMEM((D_HEADS, 1), F32), pltpu.VMEM((D_HEADS, 1), F32),
                        pltpu.VMEM((D_HEADS, width), F32), pltpu.VMEM((D_HEADS, 1), F32)])
    return pl.pallas_call(
        functools.partial(_fox_dec_kernel, npg=npg, page=page), grid_spec=grid_spec,
        out_shape=jax.ShapeDtypeStruct((ns, 1, width), F32),
        compiler_params=_cp("arbitrary", "arbitrary"), name="fox_dec")(
            pt_flat, q, kn, vn, lfn, *([kt_pool] * npg), *([vt_pool] * npg), *([lft_pool] * npg))


def _rope_tables(pos):
    half = B_ROPE // 2
    inv = ROPE_THETA ** (-jnp.arange(half, dtype=F32) / half)
    ang = pos.astype(F32)[:, None] * inv
    cos = jnp.concatenate([jnp.cos(ang)] * 2, axis=-1)
    sin = jnp.concatenate([jnp.sin(ang)] * 2, axis=-1)
    return cos, sin, jnp.tile(cos, (1, B_HEADS)), jnp.tile(sin, (1, B_HEADS))


def _rot_cols(w):
    half = w.shape[-1] // 2
    return jnp.concatenate([-w[..., half:], w[..., :half]], axis=-1)


def _prep_even(w_in, w_q_b, w_kv_b):
    d = w_in.shape[0]
    kr = w_in[:, E_KR:E_KRR]
    w_in_p = jnp.concatenate([w_in, _rot_cols(kr), jnp.zeros((d, E_COLS - E_END), w_in.dtype)], axis=1)
    qr = w_q_b.shape[0]
    wq = w_q_b.reshape(qr, B_HEADS, B_NOPE + B_ROPE)
    rp = wq[:, :, B_NOPE:]
    wq_p = jnp.concatenate([wq[:, :, :B_NOPE].reshape(qr, -1), rp.reshape(qr, -1), _rot_cols(rp).reshape(qr, -1)],
                           axis=1)
    wkn = jnp.transpose(w_kv_b[:, :, :B_NOPE], (1, 2, 0))
    wkv = jnp.transpose(w_kv_b[:, :, B_NOPE:], (1, 0, 2))
    return w_in_p.astype(BF16), wq_p.astype(BF16), wkn.astype(BF16), wkv.astype(BF16)


def kernel(x_prompt, x_sample, cache_mla_latent, cache_mla_krope, state_conv_a, cache_fox_k, cache_fox_v,
           cache_fox_logf, state_conv_c, page_table, c_prompt, c_sample, w_ada, b_ada, ln_g, ln_b,
           w_in_even, w_q_b, w_kv_b, q_norm, kv_norm, conv_a, w_out_even, w_in_odd, b_f, conv_c,
           conv_c_b, cn_g, cn_b, w_out_odd, w_ff1, w_ff2):
    depth = w_ada.shape[0]
    alpha = float((2 * depth) ** 0.25)
    nbp, seq, d = x_prompt.shape
    ns = x_sample.shape[0]
    n_pages = page_table.shape[1]
    page = cache_mla_latent.shape[2]
    past_len = n_pages * page
    n_pool = cache_fox_k.shape[1]
    aw = conv_a.shape[-1]
    cw = conv_c.shape[-1]
    dw = D_HEADS * D_HEAD_DIM

    tm = min(256, seq)
    tq_mla, tk_mla = min(256, seq), min(1024, seq)
    tq_fox, tk_fox = min(512, seq), min(512, seq)
    tm_ffn, tf = min(1024, seq), min(1024, w_ff1.shape[-1])
    npg = min(16, n_pages)
    npg_mla = min(32, n_pages)

    n_c = nbp + ns
    n_pad = -n_c % 8
    c_all = jnp.concatenate([c_prompt, c_sample, jnp.zeros((n_pad, d), F32)], axis=0)
    mods = _ada(c_all, w_ada, b_ada)
    mod_p = jnp.transpose(mods[:, :, :nbp], (0, 2, 1, 3))[:, :, :, None, :]
    mod_s = mods[:, None, :, nbp:n_c]

    tabs_p = _rope_tables(jnp.arange(seq, dtype=jnp.int32))
    tabs_s = _rope_tables(jnp.full((ns,), past_len, dtype=jnp.int32))
    pt_flat = page_table.reshape(-1)
    keys_minor = lambda a: jnp.transpose(a, (0, 1, 3, 4, 2)).reshape(a.shape[:2] + (-1, a.shape[2]))
    kt_pool = keys_minor(cache_fox_k)
    vt_pool = keys_minor(cache_fox_v)
    lft_pool = jnp.swapaxes(cache_fox_logf, 2, 3)
    krt_pool = jnp.swapaxes(cache_mla_krope, 2, 3)

    xp = x_prompt
    xs = x_sample.reshape(1, ns, d)
    even_p, even_s, odd_p, odd_s = [], [], [], []
    for l in range(depth):
        i = l // 2
        row = lambda a: a.reshape(1, -1)
        g0, b0, g1, b1 = row(ln_g[l, 0]), row(ln_b[l, 0]), row(ln_g[l, 1]), row(ln_b[l, 1])
        if l % 2 == 0:
            w_in_p, wq_p, wkn, wkv = _prep_even(w_in_even[i], w_q_b[i], w_kv_b[i])
            w_out = w_out_even[i].astype(BF16)
            tail = (conv_a[i], w_out[:aw], w_out[aw:], g0, b0)
            qn, kvn = row(q_norm[i]), row(kv_norm[i])
            u, gb, ckv, kr, kcat, qnope, qrope = _even_in(xp, mod_p[l], w_in_p, qn, kvn, wq_p, *tabs_p, tm)
            yb = _mla_attn(qnope, qrope, kcat, wkn, wkv, tq_mla, tk_mla)
            xp = _mixer_out(_even_out_kernel, u, None, (gb, yb), (_row_spec(tm, aw), _row_spec(tm, yb.shape[-1])),
                            xp, mod_p[l], tail, tm, alpha, False, A_CONV)
            even_p.append((ckv, kr, u[:, seq - (A_CONV - 1):]))
            u, gb, ckv, kr, _, qnope, qrope = _even_in(xs, mod_s[l], w_in_p, qn, kvn, wq_p, *tabs_s, ns)
            ql = jnp.transpose(_qlat(qnope[0], wkn), (1, 0, 2))
            qr = qrope[0].reshape(ns, B_HEADS, B_ROPE)
            o_lat = _mla_dec(pt_flat, ql, qr, ckv[0][:, None], kr[0][:, None], cache_mla_latent, krt_pool,
                             i, n_pages, npg_mla)
            yb = _yb(jnp.transpose(o_lat, (1, 0, 2)), wkv)[None]
            hist = jnp.transpose(state_conv_a[i], (1, 0, 2))
            xs = _mixer_out(_even_out_kernel, u, hist, (gb, yb), (_row_spec(ns, aw), _row_spec(ns, yb.shape[-1])),
                            xs, mod_s[l], tail, ns, alpha, True, A_CONV)
            even_s.append((ckv[0][:, None], kr[0][:, None],
                           jnp.concatenate([state_conv_a[i][:, 1:], u[0][:, None]], axis=1)))
        else:
            w_in_p = jnp.concatenate([w_in_odd[i], jnp.zeros((d, O_COLS - O_END), F32)], axis=1).astype(BF16)
            w_out = w_out_odd[i].astype(BF16)
            tail = (conv_c[i], row(conv_c_b[i]), row(cn_g[i]), row(cn_b[i]), w_out[:cw], w_out[cw:], g0, b0)
            bf = jnp.concatenate([b_f[i], jnp.zeros((O_COLS - O_F - D_HEADS,), F32)]).reshape(1, -1)
            u, k_t, v_t, lf, ka, q_tb, v_tb = _odd_in(xp, mod_p[l], w_in_p, bf, tm, True)
            o_t = _fox_attn(ka, q_tb, v_tb, tq_fox, tk_fox)
            xp = _mixer_out(_odd_out_kernel, u, None, (o_t,), (_time_minor_spec(dw, tm),),
                            xp, mod_p[l], tail, tm, alpha, False, C_CONV)
            state = lambda a: jnp.transpose(a.reshape(nbp, D_HEADS, D_HEAD_DIM, seq), (0, 3, 1, 2))
            odd_p.append((state(k_t), state(v_t), lf, u[:, seq - (C_CONV - 1):]))
            u, q, k, v, lf = _odd_in(xs, mod_s[l], w_in_p, bf, ns, False)
            o = _fox_dec(pt_flat, q[0][:, None], k[0][:, None], v[0][:, None], lf[0][:, :, None],
                         kt_pool, vt_pool, lft_pool, i, n_pages, npg)
            o_t = jnp.transpose(o.reshape(ns, dw)).astype(BF16)[None]
            hist = jnp.transpose(state_conv_c[i], (1, 0, 2))
            xs = _mixer_out(_odd_out_kernel, u, hist, (o_t,), (_time_minor_spec(dw, ns),),
                            xs, mod_s[l], tail, ns, alpha, True, C_CONV)
            hd = (ns, 1, D_HEADS, D_HEAD_DIM)
            odd_s.append((k[0].reshape(hd), v[0].reshape(hd), lf[0][:, None],
                          jnp.concatenate([state_conv_c[i][:, 1:], u[0][:, None]], axis=1)))
        w1, w2 = w_ff1[l].astype(BF16), w_ff2[l].astype(BF16)
        xp = _ffn(xp, mod_p[l], w1, w2, g1, b1, tm_ffn, tf, alpha)
        xs = _ffn(xs, mod_s[l], w1, w2, g1, b1, ns, tf, alpha)

    stack = lambda states: [jnp.stack(s) for s in zip(*states)]
    lat_p, kr_p, ca_p = stack(even_p)
    k_p, v_p, lf_p, cc_p = stack(odd_p)
    lat_s, kr_s, ca_s = stack(even_s)
    k_s, v_s, lf_s, cc_s = stack(odd_s)
    return (xp, xs.reshape(ns, 1, d), lat_p, kr_p, ca_p, k_p, v_p, lf_p, cc_p,
            lat_s, kr_s, ca_s, k_s, v_s, lf_s, cc_s)
```

```python
import functools

import numpy as np
import jax
import jax.numpy as jnp
from jax import lax
from jax.experimental import pallas as pl
from jax.experimental.pallas import tpu as pltpu

F32 = jnp.float32
BF16 = jnp.bfloat16

A_CONV = 3
C_CONV = 31
B_HEADS = 8
B_NOPE = 64
B_ROPE = 32
B_V = 64
D_HEADS = 8
D_HEAD_DIM = 64
ROPE_THETA = 10000.0
LOG2E = 1.4426950408889634
MLA_SCALE = (B_NOPE + B_ROPE) ** -0.5 * LOG2E
FOX_SCALE = D_HEAD_DIM ** -0.5 * LOG2E
LN_EPS = 1e-5
RMS_EPS = 1e-6
NEG = -1e30
LANE = 128
SUBLANE = 8
VMEM_LIMIT = 56 * 1024 * 1024

_NT = (((1,), (1,)), ((), ()))


def _cp(*sem):
    return pltpu.CompilerParams(dimension_semantics=sem, vmem_limit_bytes=VMEM_LIMIT)


def _ln(y, g, b):
    mu = jnp.mean(y, -1, keepdims=True)
    d = y - mu
    var = jnp.mean(d * d, -1, keepdims=True)
    return d * lax.rsqrt(var + LN_EPS) * g + b


def _rms(y, g):
    return y * lax.rsqrt(jnp.mean(y * y, -1, keepdims=True) + RMS_EPS) * g


def _split3(x):
    hi = x.astype(BF16)
    r = x - hi.astype(F32)
    mid = r.astype(BF16)
    lo = (r - mid.astype(F32)).astype(BF16)
    return hi, mid, lo


def _dot(a, b):
    return jnp.dot(a, b, preferred_element_type=F32)


def _dot_nt(a, b):
    return lax.dot_general(a, b, _NT, preferred_element_type=F32)


def _mod_spec(r, d):
    if r == 1:
        return pl.BlockSpec((1, 6, 1, d), lambda b, i, *_: (b, 0, 0, 0))
    return pl.BlockSpec((1, 6, r, d), lambda b, i, *_: (b, 0, i, 0))


def _row_spec(tm, c):
    return pl.BlockSpec((1, tm, c), lambda b, i, *_: (b, i, 0))


def _time_minor_spec(c, tm):
    return pl.BlockSpec((1, c, tm), lambda b, i, *_: (b, 0, i))


def _full_spec(shape):
    return pl.BlockSpec(shape, lambda *_: (0,) * len(shape))


def _ada_kernel(c_ref, w_ref, b_ref, o_ref):
    c = c_ref[...]
    s = (c * jax.nn.sigmoid(c)).astype(BF16)
    o_ref[0, 0] = _dot(s, w_ref[0].astype(BF16)) + b_ref[0]


def _ada(c_all, w_ada, b_ada):
    depth, d, d6 = w_ada.shape
    n = c_all.shape[0]
    return pl.pallas_call(
        _ada_kernel, grid=(depth, d6 // d),
        in_specs=[pl.BlockSpec((n, d), lambda l, j: (0, 0)),
                  pl.BlockSpec((1, d, d), lambda l, j: (l, 0, j)),
                  pl.BlockSpec((1, 1, d), lambda l, j: (l, 0, j))],
        out_specs=pl.BlockSpec((1, 1, n, d), lambda l, j: (l, j, 0, 0)),
        out_shape=jax.ShapeDtypeStruct((depth, d6 // d, n, d), F32),
        compiler_params=_cp("arbitrary", "arbitrary"), name="ada")(c_all, w_ada, b_ada.reshape(depth, 1, d6))


E_XA, E_GB, E_GC, E_QA, E_KV, E_KR, E_KRR, E_END = 0, 512, 1024, 1536, 1920, 2176, 2208, 2240
E_COLS = 2304
Q_NOPE, Q_ROPE, Q_ROT, Q_END = 0, 512, 768, 1024
MLA_RANK = E_KR - E_KV
MLA_KC = 384


def _even_in_kernel(x_ref, mod_ref, w_ref, qn_ref, kvn_ref, wq_ref, ck_ref, sk_ref, cq_ref, sq_ref,
                    u_ref, gb_ref, ckv_ref, kr_ref, kcat_ref, qnope_ref, qrope_ref):
    m = mod_ref[0]
    h = x_ref[0] * (1.0 + m[1]) + m[0]
    z = _dot(h.astype(BF16), w_ref[...])
    u_ref[0] = z[:, E_GC:E_QA] * z[:, E_XA:E_GB]
    gb_ref[0] = z[:, E_GB:E_GC]
    ckv = _rms(z[:, E_KV:E_KR], kvn_ref[...])
    ckv_ref[0] = ckv
    kr = z[:, E_KR:E_KRR] * ck_ref[...] + z[:, E_KRR:E_END] * sk_ref[...]
    kr_ref[0] = kr
    kcat_ref[0, :, 0:MLA_RANK] = ckv.astype(BF16)
    kcat_ref[0, :, MLA_RANK:MLA_RANK + B_ROPE] = kr.astype(BF16)
    kcat_ref[0, :, MLA_RANK + B_ROPE:] = jnp.zeros((kr.shape[0], MLA_KC - MLA_RANK - B_ROPE), BF16)
    qn = _rms(z[:, E_QA:E_KV], qn_ref[...])
    q = _dot(qn.astype(BF16), wq_ref[...])
    qnope_ref[0] = q[:, Q_NOPE:Q_ROPE].astype(BF16)
    qr = q[:, Q_ROPE:Q_ROT] * cq_ref[...] + q[:, Q_ROT:Q_END] * sq_ref[...]
    qrope_ref[0] = (qr * MLA_SCALE).astype(BF16)


def _even_in(x, mod, w_in, qn, kvn, wq, ck, sk, cq, sq, tm):
    nb, t, d = x.shape
    r = mod.shape[2]
    aw = E_GB - E_XA
    outs = [(aw, F32), (aw, F32), (MLA_RANK, F32), (B_ROPE, F32), (MLA_KC, BF16),
            (B_HEADS * B_NOPE, BF16), (B_HEADS * B_ROPE, BF16)]
    tab = lambda c: pl.BlockSpec((tm, c), lambda b, i: (i, 0))
    return pl.pallas_call(
        _even_in_kernel, grid=(nb, t // tm),
        in_specs=[_row_spec(tm, d), _mod_spec(r, d), _full_spec(w_in.shape), _full_spec(qn.shape),
                  _full_spec(kvn.shape), _full_spec(wq.shape), tab(B_ROPE), tab(B_ROPE),
                  tab(B_HEADS * B_ROPE), tab(B_HEADS * B_ROPE)],
        out_specs=[_row_spec(tm, c) for c, _ in outs],
        out_shape=[jax.ShapeDtypeStruct((nb, t, c), dt) for c, dt in outs],
        compiler_params=_cp("arbitrary", "arbitrary"), name="even_in")(x, mod, w_in, qn, kvn, wq, ck, sk, cq, sq)


def _causal_pairs(t, tq, tk):
    qi, kj, last = [], [], []
    for i in range(t // tq):
        n = ((i + 1) * tq - 1) // tk + 1
        for j in range(n):
            qi.append(i)
            kj.append(j)
            last.append(int(j == n - 1))
    return (np.asarray(qi, np.int32), np.asarray(kj, np.int32), np.asarray(last, np.int32))


ATTN_LOOKAHEAD = 3


def _mla_attn_kernel(qi_ref, kj_ref, last_ref, qn_ref, qr_ref, kc_ref, wkn_ref, wkv_ref, yb_ref,
                     qc_s, m_s, l_s, acc_s, *, tq, tk):
    step = pl.program_id(1)
    qi = qi_ref[step]
    kj = kj_ref[step]

    @pl.when(kj == 0)
    def _init():
        qn = qn_ref[0]
        qr = qr_ref[0]
        for h in range(B_HEADS):
            ql = _dot(qn[:, h * B_NOPE:(h + 1) * B_NOPE], wkn_ref[h]) * MLA_SCALE
            qc_s[h * tq:(h + 1) * tq, 0:MLA_RANK] = ql.astype(BF16)
            qc_s[h * tq:(h + 1) * tq, MLA_RANK:MLA_RANK + B_ROPE] = qr[:, h * B_ROPE:(h + 1) * B_ROPE]
            qc_s[h * tq:(h + 1) * tq, MLA_RANK + B_ROPE:] = jnp.zeros((tq, MLA_KC - MLA_RANK - B_ROPE), BF16)
        m_s[...] = jnp.full_like(m_s, NEG)
        l_s[...] = jnp.zeros_like(l_s)
        acc_s[...] = jnp.zeros_like(acc_s)

    def update(masked):
        kc = kc_ref[0]
        v = kc[:, :MLA_RANK]

        def scores(h):
            return _dot_nt(qc_s[h * tq:(h + 1) * tq, :], kc)

        pending = [scores(h) for h in range(ATTN_LOOKAHEAD)]
        for h in range(B_HEADS):
            rows = slice(h * tq, (h + 1) * tq)
            s = pending.pop(0)
            if h + ATTN_LOOKAHEAD < B_HEADS:
                pending.append(scores(h + ATTN_LOOKAHEAD))
            if masked:
                ri = lax.broadcasted_iota(jnp.int32, s.shape, 0)
                ci = lax.broadcasted_iota(jnp.int32, s.shape, 1)
                s = jnp.where(ri - ci >= kj * tk - qi * tq, s, NEG)
            m_prev = m_s[rows, :]
            m_new = jnp.maximum(m_prev, jnp.max(s, -1, keepdims=True))
            alpha = jnp.exp2(m_prev - m_new)
            p = jnp.exp2(s - m_new)
            l_s[rows, :] = alpha * l_s[rows, :] + jnp.sum(p, -1, keepdims=True)
            acc_s[rows, :] = alpha * acc_s[rows, :] + _dot(p.astype(BF16), v)
            m_s[rows, :] = m_new

    needs_mask = (kj + 1) * tk - 1 > qi * tq

    @pl.when(needs_mask)
    def _masked():
        update(True)

    @pl.when(jnp.logical_not(needs_mask))
    def _plain():
        update(False)

    @pl.when(last_ref[step] == 1)
    def _fin():
        o = acc_s[...] / l_s[...]
        ys = [_dot(o[h * tq:(h + 1) * tq].astype(BF16), wkv_ref[h]) for h in range(B_HEADS)]
        yb_ref[0] = jnp.concatenate(ys, axis=1).astype(yb_ref.dtype)


def _mla_attn(qnope, qrope, kcat, wkn, wkv, tq, tk):
    nb, t, _ = qnope.shape
    qi, kj, last = _causal_pairs(t, tq, tk)
    grid_spec = pltpu.PrefetchScalarGridSpec(
        num_scalar_prefetch=3, grid=(nb, len(qi)),
        in_specs=[pl.BlockSpec((1, tq, qnope.shape[-1]), lambda b, s, qi, kj, la: (b, qi[s], 0)),
                  pl.BlockSpec((1, tq, qrope.shape[-1]), lambda b, s, qi, kj, la: (b, qi[s], 0)),
                  pl.BlockSpec((1, tk, MLA_KC), lambda b, s, qi, kj, la: (b, kj[s], 0)),
                  _full_spec(wkn.shape), _full_spec(wkv.shape)],
        out_specs=pl.BlockSpec((1, tq, B_HEADS * B_V), lambda b, s, qi, kj, la: (b, qi[s], 0)),
        scratch_shapes=[pltpu.VMEM((B_HEADS * tq, MLA_KC), BF16),
                        pltpu.VMEM((B_HEADS * tq, 1), F32), pltpu.VMEM((B_HEADS * tq, 1), F32),
                        pltpu.VMEM((B_HEADS * tq, MLA_RANK), F32)])
    return pl.pallas_call(
        functools.partial(_mla_attn_kernel, tq=tq, tk=tk), grid_spec=grid_spec,
        out_shape=jax.ShapeDtypeStruct((nb, t, B_HEADS * B_V), BF16),
        compiler_params=_cp("arbitrary", "arbitrary"), name="mla_attn")(
            jnp.asarray(qi), jnp.asarray(kj), jnp.asarray(last), qnope, qrope, kcat, wkn, wkv)


CONV_ROWS = 32


def _conv_prompt(u, halo_ref, ext_s, w_ref, i, k):
    tm = u.shape[0]
    hr = halo_ref.shape[1]
    ext_s[0, 0:hr] = jnp.where(i > 0, halo_ref[0], 0.0)
    ext_s[0, hr:hr + tm] = u
    n = hr + tm - SUBLANE
    for p in range(1, SUBLANE):
        ext_s[p, 0:n] = ext_s[0, p:p + n]
    off = hr - (k - 1)
    blocks = []
    for r in range(0, tm, CONV_ROWS):
        acc = w_ref[k - 1:k] * ext_s[0, hr + r:hr + r + CONV_ROWS]
        for j in range(k - 1):
            p = (off + j) % SUBLANE
            base = off + j + r - p
            acc = acc + w_ref[j:j + 1] * ext_s[p, base:base + CONV_ROWS]
        blocks.append(acc)
    return jnp.concatenate(blocks, axis=0)


def _conv_sample(u, hist_ref, w_ref, k):
    acc = w_ref[k - 1:k] * u
    for j in range(k - 1):
        acc = acc + w_ref[j:j + 1] * hist_ref[j]
    return acc


def _halo_spec(tm, hr, c):
    return pl.BlockSpec((1, hr, c), lambda b, i: (b, jnp.maximum(i * (tm // hr) - 1, 0), 0))


def _even_out_kernel(*refs, alpha, sample):
    if sample:
        (u_ref, hist_ref, gb_ref, yb_ref, x_ref, mod_ref, cw_ref, wa_ref, wb_ref, g_ref, b_ref, o_ref) = refs
        conv = _conv_sample(u_ref[0], hist_ref, cw_ref, A_CONV)
    else:
        (u_ref, halo_ref, gb_ref, yb_ref, x_ref, mod_ref, cw_ref, wa_ref, wb_ref, g_ref, b_ref, o_ref, ext_s) = refs
        conv = _conv_prompt(u_ref[0], halo_ref, ext_s, cw_ref, pl.program_id(1), A_CONV)
    ya = gb_ref[0] * conv
    mix = _dot(ya.astype(BF16), wa_ref[...]) + _dot(yb_ref[0], wb_ref[...])
    m = mod_ref[0]
    o_ref[0] = _ln(alpha * x_ref[0] + (1.0 + m[2]) * mix, g_ref[...], b_ref[...])


def _mixer_out(kernel, u, side, others, other_specs, x, mod, tail, tm, alpha, sample, k):
    nb, t, d = x.shape
    c = u.shape[-1]
    r = mod.shape[2]
    hr = 8 if k - 1 <= 8 else 32
    if sample:
        side_arg, side_spec, scratch = side, _full_spec(side.shape), []
    else:
        side_arg, side_spec, scratch = u, _halo_spec(tm, hr, c), [pltpu.VMEM((SUBLANE, tm + hr, c), F32)]
    return pl.pallas_call(
        functools.partial(kernel, alpha=alpha, sample=sample), grid=(nb, t // tm),
        in_specs=[_row_spec(tm, c), side_spec] + list(other_specs)
        + [_row_spec(tm, d), _mod_spec(r, d)] + [_full_spec(w.shape) for w in tail],
        out_specs=_row_spec(tm, d), out_shape=jax.ShapeDtypeStruct((nb, t, d), F32),
        scratch_shapes=scratch, compiler_params=_cp("arbitrary", "arbitrary"),
        name=kernel.__name__.strip("_"))(u, side_arg, *others, x, mod, *tail)


def _ffn_kernel(x_ref, mod_ref, w1_ref, w2_ref, g_ref, b_ref, o_ref, h_s, acc_s, *, alpha):
    f = pl.program_id(2)

    @pl.when(f == 0)
    def _first():
        m = mod_ref[0]
        h_s[...] = (x_ref[0] * (1.0 + m[4]) + m[3]).astype(BF16)
        acc_s[...] = jnp.zeros_like(acc_s)

    a = jnp.square(jnp.maximum(_dot(h_s[...], w1_ref[...]), 0.0))
    acc_s[...] += _dot(a.astype(BF16), w2_ref[...])

    @pl.when(f == pl.num_programs(2) - 1)
    def _last():
        m = mod_ref[0]
        o_ref[0] = _ln(alpha * x_ref[0] + (1.0 + m[5]) * acc_s[...], g_ref[...], b_ref[...])


def _ffn(x, mod, w1, w2, g, b, tm, tf, alpha):
    nb, t, d = x.shape
    r = mod.shape[2]
    ff = w1.shape[1]
    return pl.pallas_call(
        functools.partial(_ffn_kernel, alpha=alpha), grid=(nb, t // tm, ff // tf),
        in_specs=[_row_spec(tm, d), _mod_spec(r, d),
                  pl.BlockSpec((d, tf), lambda b, i, f: (0, f)), pl.BlockSpec((tf, d), lambda b, i, f: (f, 0)),
                  _full_spec(g.shape), _full_spec(b.shape)],
        out_specs=_row_spec(tm, d), out_shape=jax.ShapeDtypeStruct((nb, t, d), F32),
        scratch_shapes=[pltpu.VMEM((tm, d), BF16), pltpu.VMEM((tm, d), F32)],
        compiler_params=_cp("arbitrary", "arbitrary", "arbitrary"), name="ffn")(x, mod, w1, w2, g, b)


O_GA, O_GB, O_Q, O_K, O_V, O_F, O_END = 0, 512, 1024, 1536, 2048, 2560, 2568
O_COLS = 2688


def _log_sigmoid(x):
    return jnp.minimum(x, 0.0) - jnp.log(1.0 + jnp.exp(-jnp.abs(x)))


FOX_KC = 128
FOX_DW = O_K - O_Q


def _fox_key_placement():
    p = np.zeros((FOX_DW + 3 * LANE, D_HEADS * FOX_KC), np.float32)
    for h in range(D_HEADS):
        for c in range(D_HEAD_DIM):
            p[h * D_HEAD_DIM + c, h * FOX_KC + c] = 1.0
        for piece in range(3):
            p[FOX_DW + piece * LANE + h, h * FOX_KC + D_HEAD_DIM + piece] = 1.0
    return jnp.asarray(p, BF16)


def _odd_in_kernel(*refs, prompt):
    if prompt:
        (x_ref, mod_ref, w_ref, bf_ref, place_ref, u_ref, kt_ref, vt_ref, lf_ref, ka_ref, qt_ref, vtb_ref,
         carry_s) = refs
    else:
        (x_ref, mod_ref, w_ref, bf_ref, u_ref, q_ref, k_ref, v_ref, lf_ref) = refs
    m = mod_ref[0]
    h = x_ref[0] * (1.0 + m[1]) + m[0]
    z = _dot(h.astype(BF16), w_ref[...])
    tm = z.shape[0]
    u_ref[0] = z[:, O_GA:O_GB] * jax.nn.sigmoid(z[:, O_GB:O_Q])
    q = z[:, O_Q:O_K] * FOX_SCALE
    k = z[:, O_K:O_V]
    v = z[:, O_V:O_F]
    lane = lax.broadcasted_iota(jnp.int32, (tm, O_COLS - O_F), 1)
    lf = jnp.where(lane < D_HEADS, _log_sigmoid(z[:, O_F:O_COLS] + bf_ref[...]), 0.0)
    lf_ref[0] = lf[:, :D_HEADS]
    if not prompt:
        q_ref[0] = q.astype(BF16)
        k_ref[0] = k
        v_ref[0] = v
        return

    kt_ref[0] = k.T
    vt = v.T
    vt_ref[0] = vt
    vtb_ref[0] = vt.astype(BF16)
    qt_ref[0] = q.T.astype(BF16)

    @pl.when(pl.program_id(1) == 0)
    def _reset():
        carry_s[...] = jnp.zeros_like(carry_s)

    tri = (lax.broadcasted_iota(jnp.int32, (tm, tm), 0) >= lax.broadcasted_iota(jnp.int32, (tm, tm), 1))
    tri = jnp.where(tri, 1.0, 0.0).astype(BF16)
    cum = sum(_dot(tri, piece) for piece in _split3(lf)) + carry_s[...]
    carry_s[...] = cum[tm - 1:tm]
    cat = jnp.concatenate((k.astype(BF16),) + _split3(cum * LOG2E), axis=1)
    ka_ref[0] = _dot(cat, place_ref[...]).astype(BF16)


def _odd_in(x, mod, w_in, bf, tm, prompt):
    nb, t, d = x.shape
    r = mod.shape[2]
    cw = O_GB - O_GA
    ins = [x, mod, w_in, bf]
    in_specs = [_row_spec(tm, d), _mod_spec(r, d), _full_spec(w_in.shape), _full_spec(bf.shape)]
    if prompt:
        place = _fox_key_placement()
        ins.append(place)
        in_specs.append(_full_spec(place.shape))
        rows = [(cw, F32)]
        cols = [(FOX_DW, F32), (FOX_DW, F32)]
        rows2 = [(D_HEADS, F32), (D_HEADS * FOX_KC, BF16)]
        cols2 = [(FOX_DW, BF16), (FOX_DW, BF16)]
        out_specs = ([_row_spec(tm, c) for c, _ in rows] + [_time_minor_spec(c, tm) for c, _ in cols]
                     + [_row_spec(tm, c) for c, _ in rows2] + [_time_minor_spec(c, tm) for c, _ in cols2])
        out_shape = ([jax.ShapeDtypeStruct((nb, t, c), dt) for c, dt in rows]
                     + [jax.ShapeDtypeStruct((nb, c, t), dt) for c, dt in cols]
                     + [jax.ShapeDtypeStruct((nb, t, c), dt) for c, dt in rows2]
                     + [jax.ShapeDtypeStruct((nb, c, t), dt) for c, dt in cols2])
        scratch = [pltpu.VMEM((1, O_COLS - O_F), F32)]
    else:
        outs = [(cw, F32), (FOX_DW, BF16), (FOX_DW, F32), (FOX_DW, F32), (D_HEADS, F32)]
        out_specs = [_row_spec(tm, c) for c, _ in outs]
        out_shape = [jax.ShapeDtypeStruct((nb, t, c), dt) for c, dt in outs]
        scratch = []
    return pl.pallas_call(
        functools.partial(_odd_in_kernel, prompt=prompt), grid=(nb, t // tm),
        in_specs=in_specs, out_specs=out_specs, out_shape=out_shape,
        scratch_shapes=scratch, compiler_params=_cp("arbitrary", "arbitrary"), name="odd_in")(*ins)


def _fox_attn_kernel(qi_ref, kj_ref, last_ref, ka_ref, qt_ref, vt_ref, o_ref, qa_s, m_s, l_s, acc_s, *, tq, tk):
    step = pl.program_id(1)
    qi = qi_ref[step]
    kj = kj_ref[step]
    hd = D_HEAD_DIM

    @pl.when(kj == 0)
    def _init():
        minus_ones = jnp.where(lax.broadcasted_iota(jnp.int32, (FOX_KC - hd, tq), 0) < 3, -1.0, 0.0).astype(BF16)
        for h in range(D_HEADS):
            qa_s[h, 0:hd, :] = qt_ref[0, h * hd:(h + 1) * hd, :]
            qa_s[h, hd:FOX_KC, :] = minus_ones
        m_s[...] = jnp.full_like(m_s, NEG)
        l_s[...] = jnp.zeros_like(l_s)
        acc_s[...] = jnp.zeros_like(acc_s)

    def update(masked):
        def scores(h):
            return _dot(ka_ref[0, :, h * FOX_KC:(h + 1) * FOX_KC], qa_s[h])

        pending = [scores(h) for h in range(ATTN_LOOKAHEAD)]
        for h in range(D_HEADS):
            s = pending.pop(0)
            if h + ATTN_LOOKAHEAD < D_HEADS:
                pending.append(scores(h + ATTN_LOOKAHEAD))
            if masked:
                ki = lax.broadcasted_iota(jnp.int32, s.shape, 0)
                qq = lax.broadcasted_iota(jnp.int32, s.shape, 1)
                s = jnp.where(qq - ki >= kj * tk - qi * tq, s, NEG)
            m_prev = m_s[h]
            m_new = jnp.maximum(m_prev, jnp.max(s, 0, keepdims=True))
            alpha = jnp.exp2(m_prev - m_new)
            p = jnp.exp2(s - m_new)
            l_s[h] = alpha * l_s[h] + jnp.sum(p, 0, keepdims=True)
            acc_s[h] = alpha * acc_s[h] + _dot(vt_ref[0, h * hd:(h + 1) * hd, :], p.astype(BF16))
            m_s[h] = m_new

    needs_mask = (kj + 1) * tk - 1 > qi * tq

    @pl.when(needs_mask)
    def _masked():
        update(True)

    @pl.when(jnp.logical_not(needs_mask))
    def _plain():
        update(False)

    @pl.when(last_ref[step] == 1)
    def _fin():
        for h in range(D_HEADS):
            o_ref[0, h * hd:(h + 1) * hd, :] = (acc_s[h] / l_s[h]).astype(o_ref.dtype)


def _fox_attn(ka, q_t, v_t, tq, tk):
    nb, t, _ = ka.shape
    dw = v_t.shape[1]
    qi, kj, last = _causal_pairs(t, tq, tk)
    qmap = lambda b, s, qi, kj, la: (b, 0, qi[s])
    grid_spec = pltpu.PrefetchScalarGridSpec(
        num_scalar_prefetch=3, grid=(nb, len(qi)),
        in_specs=[pl.BlockSpec((1, tk, ka.shape[-1]), lambda b, s, qi, kj, la: (b, kj[s], 0)),
                  pl.BlockSpec((1, dw, tq), qmap),
                  pl.BlockSpec((1, dw, tk), lambda b, s, qi, kj, la: (b, 0, kj[s]))],
        out_specs=pl.BlockSpec((1, dw, tq), qmap),
        scratch_shapes=[pltpu.VMEM((D_HEADS, FOX_KC, tq), BF16), pltpu.VMEM((D_HEADS, 1, tq), F32),
                        pltpu.VMEM((D_HEADS, 1, tq), F32), pltpu.VMEM((D_HEADS, D_HEAD_DIM, tq), F32)])
    return pl.pallas_call(
        functools.partial(_fox_attn_kernel, tq=tq, tk=tk), grid_spec=grid_spec,
        out_shape=jax.ShapeDtypeStruct((nb, dw, t), BF16),
        compiler_params=_cp("arbitrary", "arbitrary"), name="fox_attn")(
            jnp.asarray(qi), jnp.asarray(kj), jnp.asarray(last), ka, q_t, v_t)


def _odd_out_kernel(*refs, alpha, sample):
    if sample:
        (u_ref, hist_ref, o_in_ref, x_ref, mod_ref, cw_ref, cb_ref, cg_ref, cbb_ref, wa_ref, wb_ref, g_ref, b_ref,
         o_ref) = refs
        conv = _conv_sample(u_ref[0], hist_ref, cw_ref, C_CONV)
    else:
        (u_ref, halo_ref, o_in_ref, x_ref, mod_ref, cw_ref, cb_ref, cg_ref, cbb_ref, wa_ref, wb_ref, g_ref, b_ref,
         o_ref, ext_s) = refs
        conv = _conv_prompt(u_ref[0], halo_ref, ext_s, cw_ref, pl.program_id(1), C_CONV)
    y = _ln(conv + cb_ref[...], cg_ref[...], cbb_ref[...])
    yc = y * jax.nn.sigmoid(y)
    attn = lax.dot_general(o_in_ref[0], wb_ref[...], (((0,), (0,)), ((), ())), preferred_element_type=F32)
    mix = _dot(yc.astype(BF16), wa_ref[...]) + attn
    m = mod_ref[0]
    o_ref[0] = _ln(alpha * x_ref[0] + (1.0 + m[2]) * mix, g_ref[...], b_ref[...])


def _qlat_kernel(qn_ref, wkn_ref, o_ref):
    qn = qn_ref[...]
    for h in range(B_HEADS):
        o_ref[h] = (_dot(qn[:, h * B_NOPE:(h + 1) * B_NOPE], wkn_ref[h]) * MLA_SCALE).astype(BF16)


def _qlat(qnope, wkn):
    n = qnope.shape[0]
    return pl.pallas_call(_qlat_kernel, out_shape=jax.ShapeDtypeStruct((B_HEADS, n, wkn.shape[-1]), BF16),
                          name="qlat")(qnope, wkn)


def _yb_kernel(o_ref, wkv_ref, y_ref):
    ys = [_dot(o_ref[h].astype(BF16), wkv_ref[h]) for h in range(B_HEADS)]
    y_ref[...] = jnp.concatenate(ys, axis=1).astype(y_ref.dtype)


def _yb(o_lat_t, wkv):
    n = o_lat_t.shape[1]
    return pl.pallas_call(_yb_kernel, out_shape=jax.ShapeDtypeStruct((n, B_HEADS * B_V), BF16),
                          name="yb")(o_lat_t, wkv)


def _online_update(s, m_s, l_s, acc_s, values, values_transposed):
    m_prev = m_s[...]
    m_new = jnp.maximum(m_prev, jnp.max(s, -1, keepdims=True))
    alpha = jnp.exp2(m_prev - m_new)
    p = jnp.exp2(s - m_new)
    l_s[...] = alpha * l_s[...] + jnp.sum(p, -1, keepdims=True)
    pv = _dot_nt(p.astype(BF16), values) if values_transposed else _dot(p.astype(BF16), values)
    acc_s[...] = alpha * acc_s[...] + pv
    m_s[...] = m_new


def _mla_dec_kernel(pt_ref, ql_ref, qr_ref, cn_ref, kn_ref, *rest, npg, page):
    lat_refs, kr_refs = rest[:npg], rest[npg:2 * npg]
    o_ref, lat_s, kr_s, m_s, l_s, acc_s = rest[2 * npg:]
    c = pl.program_id(1)

    @pl.when(c == 0)
    def _init():
        m_s[...] = jnp.full_like(m_s, NEG)
        l_s[...] = jnp.zeros_like(l_s)
        acc_s[...] = jnp.zeros_like(acc_s)

    for i in range(npg):
        lat_s[i * page:(i + 1) * page, :] = lat_refs[i][0, 0].astype(BF16)
        kr_s[:, i * page:(i + 1) * page] = kr_refs[i][0, 0].astype(BF16)
    ql = ql_ref[0]
    qr = qr_ref[0]
    s = _dot_nt(ql, lat_s[...]) + _dot(qr, kr_s[...])
    _online_update(s, m_s, l_s, acc_s, lat_s[...], False)

    @pl.when(c == pl.num_programs(1) - 1)
    def _fin():
        cn = cn_ref[0].astype(BF16).astype(F32)
        kn = kn_ref[0].astype(BF16).astype(F32)
        s_new = (jnp.sum(ql.astype(F32) * cn, -1, keepdims=True) + jnp.sum(qr.astype(F32) * kn, -1, keepdims=True))
        m_prev = m_s[...]
        m_new = jnp.maximum(m_prev, s_new)
        alpha = jnp.exp2(m_prev - m_new)
        p_new = jnp.exp2(s_new - m_new)
        o_ref[0] = (alpha * acc_s[...] + p_new * cn) / (alpha * l_s[...] + p_new)


def _page_map(layer, n_pages, npg, i):
    return lambda b, c, pt: (layer, pt[b * n_pages + c * npg + i], 0, 0)


def _mla_dec(pt_flat, ql, qr, cn, kn, lat_pool, krt_pool, layer, n_pages, npg):
    ns = ql.shape[0]
    page, rank = lat_pool.shape[2], lat_pool.shape[3]
    seq = lambda shape: pl.BlockSpec((1,) + shape, lambda b, c, pt: (b, 0, 0))
    grid_spec = pltpu.PrefetchScalarGridSpec(
        num_scalar_prefetch=1, grid=(ns, n_pages // npg),
        in_specs=[seq((B_HEADS, rank)), seq((B_HEADS, B_ROPE)), seq((1, rank)), seq((1, B_ROPE))]
        + [pl.BlockSpec((1, 1, page, rank), _page_map(layer, n_pages, npg, i)) for i in range(npg)]
        + [pl.BlockSpec((1, 1, B_ROPE, page), _page_map(layer, n_pages, npg, i)) for i in range(npg)],
        out_specs=seq((B_HEADS, rank)),
        scratch_shapes=[pltpu.VMEM((npg * page, rank), BF16), pltpu.VMEM((B_ROPE, npg * page), BF16),
                        pltpu.VMEM((B_HEADS, 1), F32), pltpu.VMEM((B_HEADS, 1), F32),
                        pltpu.VMEM((B_HEADS, rank), F32)])
    return pl.pallas_call(
        functools.partial(_mla_dec_kernel, npg=npg, page=page), grid_spec=grid_spec,
        out_shape=jax.ShapeDtypeStruct((ns, B_HEADS, rank), F32),
        compiler_params=_cp("arbitrary", "arbitrary"), name="mla_dec")(
            pt_flat, ql, qr, cn, kn, *([lat_pool] * npg), *([krt_pool] * npg))


def _fox_dec_kernel(pt_ref, q_ref, kn_ref, vn_ref, lfn_ref, *rest, npg, page):
    k_refs, v_refs, lf_refs = rest[:npg], rest[npg:2 * npg], rest[2 * npg:3 * npg]
    o_ref, k_s, v_s, m_s, l_s, acc_s, carry_s = rest[3 * npg:]
    c = pl.program_id(1)
    nh = D_HEADS
    width = nh * D_HEAD_DIM

    @pl.when(c == 0)
    def _init():
        m_s[...] = jnp.full_like(m_s, NEG)
        l_s[...] = jnp.zeros_like(l_s)
        acc_s[...] = jnp.zeros_like(acc_s)
        carry_s[...] = jnp.zeros_like(carry_s)

    for i in range(npg):
        k_s[:, i * page:(i + 1) * page] = k_refs[i][0, 0].astype(BF16)
        v_s[:, i * page:(i + 1) * page] = v_refs[i][0, 0].astype(BF16)

    diag = (lax.broadcasted_iota(jnp.int32, (nh, width), 1) // D_HEAD_DIM
            == lax.broadcasted_iota(jnp.int32, (nh, width), 0))
    qbd = jnp.where(diag, q_ref[0].astype(F32), 0.0)

    rows = nh * npg
    y = jnp.concatenate([lf_refs[i][0, 0] for i in range(npg)], axis=0)
    upper = (lax.broadcasted_iota(jnp.int32, (page, page), 0) <= lax.broadcasted_iota(jnp.int32, (page, page), 1))
    upper = jnp.where(upper, 1.0, 0.0).astype(BF16)
    z = sum(_dot(piece, upper) for piece in _split3(y))
    r0 = lax.broadcasted_iota(jnp.int32, (rows, rows), 0)
    r1 = lax.broadcasted_iota(jnp.int32, (rows, rows), 1)
    earlier = jnp.where(((r0 % nh) == (r1 % nh)) & (r1 // nh < r0 // nh), 1.0, 0.0).astype(BF16)
    totals = jnp.broadcast_to(z[:, page - 1:page], (rows, page))
    off = sum(_dot(earlier, piece) for piece in _split3(totals))
    cum = z + off + jnp.concatenate([carry_s[...]] * npg, axis=0)
    carry_s[...] = cum[rows - nh:rows, page - 1:page]
    bias = jnp.concatenate([cum[nh * i:nh * (i + 1), :] for i in range(npg)], axis=1)

    s = _dot(qbd.astype(BF16), k_s[...]) - bias * LOG2E
    _online_update(s, m_s, l_s, acc_s, v_s[...], True)

    @pl.when(c == pl.num_programs(1) - 1)
    def _fin():
        kn = kn_ref[0].astype(BF16).astype(F32)
        vn = vn_ref[0].astype(BF16).astype(F32)
        s_new = jnp.sum(qbd * kn, -1, keepdims=True) - (carry_s[...] + lfn_ref[0]) * LOG2E
        m_prev = m_s[...]
        m_new = jnp.maximum(m_prev, s_new)
        alpha = jnp.exp2(m_prev - m_new)
        p_new = jnp.exp2(s_new - m_new)
        o = (alpha * acc_s[...] + p_new * vn) / (alpha * l_s[...] + p_new)
        o_ref[0] = jnp.sum(jnp.where(diag, o, 0.0), axis=0, keepdims=True)


def _fox_dec(pt_flat, q, kn, vn, lfn, kt_pool, vt_pool, lft_pool, layer, n_pages, npg):
    ns = q.shape[0]
    width, page = kt_pool.shape[2], kt_pool.shape[3]
    seq = lambda shape: pl.BlockSpec((1,) + shape, lambda b, c, pt: (b, 0, 0))
    pages = lambda shape: [pl.BlockSpec((1, 1) + shape, _page_map(layer, n_pages, npg, i)) for i in range(npg)]
    grid_spec = pltpu.PrefetchScalarGridSpec(
        num_scalar_prefetch=1, grid=(ns, n_pages // npg),
        in_specs=[seq((1, width)), seq((1, width)), seq((1, width)), seq((D_HEADS, 1))]
        + pages((width, page)) + pages((width, page)) + pages((D_HEADS, page)),
        out_specs=seq((1, width)),
        scratch_shapes=[pltpu.VMEM((width, npg * page), BF16), pltpu.VMEM((width, npg * page), BF16),
                        pltpu.VMEM((D_HEADS, 1), F32), pltpu.VMEM((D_HEADS, 1), F32),
                        pltpu.VMEM((D_HEADS, width), F32), pltpu.VMEM((D_HEADS, 1), F32)])
    return pl.pallas_call(
        functools.partial(_fox_dec_kernel, npg=npg, page=page), grid_spec=grid_spec,
        out_shape=jax.ShapeDtypeStruct((ns, 1, width), F32),
        compiler_params=_cp("arbitrary", "arbitrary"), name="fox_dec")(
            pt_flat, q, kn, vn, lfn, *([kt_pool] * npg), *([vt_pool] * npg), *([lft_pool] * npg))


def _rope_tables(pos):
    half = B_ROPE // 2
    inv = ROPE_THETA ** (-jnp.arange(half, dtype=F32) / half)
    ang = pos.astype(F32)[:, None] * inv
    cos = jnp.concatenate([jnp.cos(ang)] * 2, axis=-1)
    sin = jnp.concatenate([jnp.sin(ang)] * 2, axis=-1)
    return cos, sin, jnp.tile(cos, (1, B_HEADS)), jnp.tile(sin, (1, B_HEADS))


def _rot_cols(w):
    half = w.shape[-1] // 2
    return jnp.concatenate([-w[..., half:], w[..., :half]], axis=-1)


def _prep_even(w_in, w_q_b, w_kv_b):
    d = w_in.shape[0]
    kr = w_in[:, E_KR:E_KRR]
    w_in_p = jnp.concatenate([w_in, _rot_cols(kr), jnp.zeros((d, E_COLS - E_END), w_in.dtype)], axis=1)
    qr = w_q_b.shape[0]
    wq = w_q_b.reshape(qr, B_HEADS, B_NOPE + B_ROPE)
    rp = wq[:, :, B_NOPE:]
    wq_p = jnp.concatenate([wq[:, :, :B_NOPE].reshape(qr, -1), rp.reshape(qr, -1), _rot_cols(rp).reshape(qr, -1)],
                           axis=1)
    wkn = jnp.transpose(w_kv_b[:, :, :B_NOPE], (1, 2, 0))
    wkv = jnp.transpose(w_kv_b[:, :, B_NOPE:], (1, 0, 2))
    return w_in_p.astype(BF16), wq_p.astype(BF16), wkn.astype(BF16), wkv.astype(BF16)


def kernel(x_prompt, x_sample, cache_mla_latent, cache_mla_krope, state_conv_a, cache_fox_k, cache_fox_v,
           cache_fox_logf, state_conv_c, page_table, c_prompt, c_sample, w_ada, b_ada, ln_g, ln_b,
           w_in_even, w_q_b, w_kv_b, q_norm, kv_norm, conv_a, w_out_even, w_in_odd, b_f, conv_c,
           conv_c_b, cn_g, cn_b, w_out_odd, w_ff1, w_ff2):
    depth = w_ada.shape[0]
    alpha = float((2 * depth) ** 0.25)
    nbp, seq, d = x_prompt.shape
    ns = x_sample.shape[0]
    n_pages = page_table.shape[1]
    page = cache_mla_latent.shape[2]
    past_len = n_pages * page
    n_pool = cache_fox_k.shape[1]
    aw = conv_a.shape[-1]
    cw = conv_c.shape[-1]
    dw = D_HEADS * D_HEAD_DIM

    tm = min(256, seq)
    tq_mla, tk_mla = min(256, seq), min(1024, seq)
    tq_fox, tk_fox = min(512, seq), min(512, seq)
    tm_ffn, tf = min(1024, seq), min(1024, w_ff1.shape[-1])
    npg = min(32, n_pages)

    n_c = nbp + ns
    n_pad = -n_c % 8
    c_all = jnp.concatenate([c_prompt, c_sample, jnp.zeros((n_pad, d), F32)], axis=0)
    mods = _ada(c_all, w_ada, b_ada)
    mod_p = jnp.transpose(mods[:, :, :nbp], (0, 2, 1, 3))[:, :, :, None, :]
    mod_s = mods[:, None, :, nbp:n_c]

    tabs_p = _rope_tables(jnp.arange(seq, dtype=jnp.int32))
    tabs_s = _rope_tables(jnp.full((ns,), past_len, dtype=jnp.int32))
    pt_flat = page_table.reshape(-1)
    keys_minor = lambda a: jnp.transpose(a, (0, 1, 3, 4, 2)).reshape(a.shape[:2] + (-1, a.shape[2]))
    kt_pool = keys_minor(cache_fox_k)
    vt_pool = keys_minor(cache_fox_v)
    lft_pool = jnp.swapaxes(cache_fox_logf, 2, 3)
    krt_pool = jnp.swapaxes(cache_mla_krope, 2, 3)

    xp = x_prompt
    xs = x_sample.reshape(1, ns, d)
    even_p, even_s, odd_p, odd_s = [], [], [], []
    for l in range(depth):
        i = l // 2
        row = lambda a: a.reshape(1, -1)
        g0, b0, g1, b1 = row(ln_g[l, 0]), row(ln_b[l, 0]), row(ln_g[l, 1]), row(ln_b[l, 1])
        if l % 2 == 0:
            w_in_p, wq_p, wkn, wkv = _prep_even(w_in_even[i], w_q_b[i], w_kv_b[i])
            w_out = w_out_even[i].astype(BF16)
            tail = (conv_a[i], w_out[:aw], w_out[aw:], g0, b0)
            qn, kvn = row(q_norm[i]), row(kv_norm[i])
            u, gb, ckv, kr, kcat, qnope, qrope = _even_in(xp, mod_p[l], w_in_p, qn, kvn, wq_p, *tabs_p, tm)
            yb = _mla_attn(qnope, qrope, kcat, wkn, wkv, tq_mla, tk_mla)
            xp = _mixer_out(_even_out_kernel, u, None, (gb, yb), (_row_spec(tm, aw), _row_spec(tm, yb.shape[-1])),
                            xp, mod_p[l], tail, tm, alpha, False, A_CONV)
            even_p.append((ckv, kr, u[:, seq - (A_CONV - 1):]))
            u, gb, ckv, kr, _, qnope, qrope = _even_in(xs, mod_s[l], w_in_p, qn, kvn, wq_p, *tabs_s, ns)
            ql = jnp.transpose(_qlat(qnope[0], wkn), (1, 0, 2))
            qr = qrope[0].reshape(ns, B_HEADS, B_ROPE)
            o_lat = _mla_dec(pt_flat, ql, qr, ckv[0][:, None], kr[0][:, None], cache_mla_latent, krt_pool,
                             i, n_pages, npg)
            yb = _yb(jnp.transpose(o_lat, (1, 0, 2)), wkv)[None]
            hist = jnp.transpose(state_conv_a[i], (1, 0, 2))
            xs = _mixer_out(_even_out_kernel, u, hist, (gb, yb), (_row_spec(ns, aw), _row_spec(ns, yb.shape[-1])),
                            xs, mod_s[l], tail, ns, alpha, True, A_CONV)
            even_s.append((ckv[0][:, None], kr[0][:, None],
                           jnp.concatenate([state_conv_a[i][:, 1:], u[0][:, None]], axis=1)))
        else:
            w_in_p = jnp.concatenate([w_in_odd[i], jnp.zeros((d, O_COLS - O_END), F32)], axis=1).astype(BF16)
            w_out = w_out_odd[i].astype(BF16)
            tail = (conv_c[i], row(conv_c_b[i]), row(cn_g[i]), row(cn_b[i]), w_out[:cw], w_out[cw:], g0, b0)
            bf = jnp.concatenate([b_f[i], jnp.zeros((O_COLS - O_F - D_HEADS,), F32)]).reshape(1, -1)
            u, k_t, v_t, lf, ka, q_tb, v_tb = _odd_in(xp, mod_p[l], w_in_p, bf, tm, True)
            o_t = _fox_attn(ka, q_tb, v_tb, tq_fox, tk_fox)
            xp = _mixer_out(_odd_out_kernel, u, None, (o_t,), (_time_minor_spec(dw, tm),),
                            xp, mod_p[l], tail, tm, alpha, False, C_CONV)
            state = lambda a: jnp.transpose(a.reshape(nbp, D_HEADS, D_HEAD_DIM, seq), (0, 3, 1, 2))
            odd_p.append((state(k_t), state(v_t), lf, u[:, seq - (C_CONV - 1):]))
            u, q, k, v, lf = _odd_in(xs, mod_s[l], w_in_p, bf, ns, False)
            o = _fox_dec(pt_flat, q[0][:, None], k[0][:, None], v[0][:, None], lf[0][:, :, None],
                         kt_pool, vt_pool, lft_pool, i, n_pages, npg)
            o_t = jnp.transpose(o.reshape(ns, dw)).astype(BF16)[None]
            hist = jnp.transpose(state_conv_c[i], (1, 0, 2))
            xs = _mixer_out(_odd_out_kernel, u, hist, (o_t,), (_time_minor_spec(dw, ns),),
                            xs, mod_s[l], tail, ns, alpha, True, C_CONV)
            hd = (ns, 1, D_HEADS, D_HEAD_DIM)
            odd_s.append((k[0].reshape(hd), v[0].reshape(hd), lf[0][:, None],
                          jnp.concatenate([state_conv_c[i][:, 1:], u[0][:, None]], axis=1)))
        w1, w2 = w_ff1[l].astype(BF16), w_ff2[l].astype(BF16)
        xp = _ffn(xp, mod_p[l], w1, w2, g1, b1, tm_ffn, tf, alpha)
        xs = _ffn(xs, mod_s[l], w1, w2, g1, b1, ns, tf, alpha)

    stack = lambda states: [jnp.stack(s) for s in zip(*states)]
    lat_p, kr_p, ca_p = stack(even_p)
    k_p, v_p, lf_p, cc_p = stack(odd_p)
    lat_s, kr_s, ca_s = stack(even_s)
    k_s, v_s, lf_s, cc_s = stack(odd_s)
    return (xp, xs.reshape(ns, 1, d), lat_p, kr_p, ca_p, k_p, v_p, lf_p, cc_p,
            lat_s, kr_s, ca_s, k_s, v_s, lf_s, cc_s)
```

```python
import functools

import numpy as np
import jax
import jax.numpy as jnp
from jax import lax
from jax.experimental import pallas as pl
from jax.experimental.pallas import tpu as pltpu

F32 = jnp.float32
BF16 = jnp.bfloat16

A_CONV = 3
C_CONV = 31
B_HEADS = 8
B_NOPE = 64
B_ROPE = 32
B_V = 64
D_HEADS = 8
D_HEAD_DIM = 64
ROPE_THETA = 10000.0
LOG2E = 1.4426950408889634
MLA_SCALE = (B_NOPE + B_ROPE) ** -0.5 * LOG2E
FOX_SCALE = D_HEAD_DIM ** -0.5 * LOG2E
LN_EPS = 1e-5
RMS_EPS = 1e-6
NEG = -1e30
LANE = 128
SUBLANE = 8
VMEM_LIMIT = 56 * 1024 * 1024

_NT = (((1,), (1,)), ((), ()))


def _cp(*sem):
    return pltpu.CompilerParams(dimension_semantics=sem, vmem_limit_bytes=VMEM_LIMIT)


def _ln(y, g, b):
    mu = jnp.mean(y, -1, keepdims=True)
    d = y - mu
    var = jnp.mean(d * d, -1, keepdims=True)
    return d * lax.rsqrt(var + LN_EPS) * g + b


def _rms(y, g):
    return y * lax.rsqrt(jnp.mean(y * y, -1, keepdims=True) + RMS_EPS) * g


def _split3(x):
    hi = x.astype(BF16)
    r = x - hi.astype(F32)
    mid = r.astype(BF16)
    lo = (r - mid.astype(F32)).astype(BF16)
    return hi, mid, lo


def _dot(a, b):
    return jnp.dot(a, b, preferred_element_type=F32)


def _dot_nt(a, b):
    return lax.dot_general(a, b, _NT, preferred_element_type=F32)


def _mod_spec(r, d):
    if r == 1:
        return pl.BlockSpec((1, 6, 1, d), lambda b, i, *_: (b, 0, 0, 0))
    return pl.BlockSpec((1, 6, r, d), lambda b, i, *_: (b, 0, i, 0))


def _row_spec(tm, c):
    return pl.BlockSpec((1, tm, c), lambda b, i, *_: (b, i, 0))


def _time_minor_spec(c, tm):
    return pl.BlockSpec((1, c, tm), lambda b, i, *_: (b, 0, i))


def _full_spec(shape):
    return pl.BlockSpec(shape, lambda *_: (0,) * len(shape))


def _ada_kernel(c_ref, w_ref, b_ref, o_ref):
    c = c_ref[...]
    s = (c * jax.nn.sigmoid(c)).astype(BF16)
    o_ref[0, 0] = _dot(s, w_ref[0].astype(BF16)) + b_ref[0]


def _ada(c_all, w_ada, b_ada):
    depth, d, d6 = w_ada.shape
    n = c_all.shape[0]
    return pl.pallas_call(
        _ada_kernel, grid=(depth, d6 // d),
        in_specs=[pl.BlockSpec((n, d), lambda l, j: (0, 0)),
                  pl.BlockSpec((1, d, d), lambda l, j: (l, 0, j)),
                  pl.BlockSpec((1, 1, d), lambda l, j: (l, 0, j))],
        out_specs=pl.BlockSpec((1, 1, n, d), lambda l, j: (l, j, 0, 0)),
        out_shape=jax.ShapeDtypeStruct((depth, d6 // d, n, d), F32),
        compiler_params=_cp("arbitrary", "arbitrary"), name="ada")(c_all, w_ada, b_ada.reshape(depth, 1, d6))


E_XA, E_GB, E_GC, E_QA, E_KV, E_KR, E_KRR, E_END = 0, 512, 1024, 1536, 1920, 2176, 2208, 2240
E_COLS = 2304
Q_NOPE, Q_ROPE, Q_ROT, Q_END = 0, 512, 768, 1024
MLA_RANK = E_KR - E_KV
MLA_KC = 384


def _even_in_kernel(x_ref, mod_ref, w_ref, qn_ref, kvn_ref, wq_ref, ck_ref, sk_ref, cq_ref, sq_ref,
                    u_ref, gb_ref, ckv_ref, kr_ref, kcat_ref, qnope_ref, qrope_ref):
    m = mod_ref[0]
    h = x_ref[0] * (1.0 + m[1]) + m[0]
    z = _dot(h.astype(BF16), w_ref[...])
    u_ref[0] = z[:, E_GC:E_QA] * z[:, E_XA:E_GB]
    gb_ref[0] = z[:, E_GB:E_GC]
    ckv = _rms(z[:, E_KV:E_KR], kvn_ref[...])
    ckv_ref[0] = ckv
    kr = z[:, E_KR:E_KRR] * ck_ref[...] + z[:, E_KRR:E_END] * sk_ref[...]
    kr_ref[0] = kr
    kcat_ref[0, :, 0:MLA_RANK] = ckv.astype(BF16)
    kcat_ref[0, :, MLA_RANK:MLA_RANK + B_ROPE] = kr.astype(BF16)
    kcat_ref[0, :, MLA_RANK + B_ROPE:] = jnp.zeros((kr.shape[0], MLA_KC - MLA_RANK - B_ROPE), BF16)
    qn = _rms(z[:, E_QA:E_KV], qn_ref[...])
    q = _dot(qn.astype(BF16), wq_ref[...])
    qnope_ref[0] = q[:, Q_NOPE:Q_ROPE].astype(BF16)
    qr = q[:, Q_ROPE:Q_ROT] * cq_ref[...] + q[:, Q_ROT:Q_END] * sq_ref[...]
    qrope_ref[0] = (qr * MLA_SCALE).astype(BF16)


def _even_in(x, mod, w_in, qn, kvn, wq, ck, sk, cq, sq, tm):
    nb, t, d = x.shape
    r = mod.shape[2]
    aw = E_GB - E_XA
    outs = [(aw, F32), (aw, F32), (MLA_RANK, F32), (B_ROPE, F32), (MLA_KC, BF16),
            (B_HEADS * B_NOPE, BF16), (B_HEADS * B_ROPE, BF16)]
    tab = lambda c: pl.BlockSpec((tm, c), lambda b, i: (i, 0))
    return pl.pallas_call(
        _even_in_kernel, grid=(nb, t // tm),
        in_specs=[_row_spec(tm, d), _mod_spec(r, d), _full_spec(w_in.shape), _full_spec(qn.shape),
                  _full_spec(kvn.shape), _full_spec(wq.shape), tab(B_ROPE), tab(B_ROPE),
                  tab(B_HEADS * B_ROPE), tab(B_HEADS * B_ROPE)],
        out_specs=[_row_spec(tm, c) for c, _ in outs],
        out_shape=[jax.ShapeDtypeStruct((nb, t, c), dt) for c, dt in outs],
        compiler_params=_cp("arbitrary", "arbitrary"), name="even_in")(x, mod, w_in, qn, kvn, wq, ck, sk, cq, sq)


def _causal_pairs(t, tq, tk):
    qi, kj, last = [], [], []
    for i in range(t // tq):
        n = ((i + 1) * tq - 1) // tk + 1
        for j in range(n):
            qi.append(i)
            kj.append(j)
            last.append(int(j == n - 1))
    return (np.asarray(qi, np.int32), np.asarray(kj, np.int32), np.asarray(last, np.int32))


ATTN_LOOKAHEAD = 3


def _mla_attn_kernel(qi_ref, kj_ref, last_ref, qn_ref, qr_ref, kc_ref, wkn_ref, wkv_ref, yb_ref,
                     qc_s, m_s, l_s, acc_s, *, tq, tk):
    step = pl.program_id(1)
    qi = qi_ref[step]
    kj = kj_ref[step]

    @pl.when(kj == 0)
    def _init():
        qn = qn_ref[0]
        qr = qr_ref[0]
        for h in range(B_HEADS):
            ql = _dot(qn[:, h * B_NOPE:(h + 1) * B_NOPE], wkn_ref[h]) * MLA_SCALE
            qc_s[h * tq:(h + 1) * tq, 0:MLA_RANK] = ql.astype(BF16)
            qc_s[h * tq:(h + 1) * tq, MLA_RANK:MLA_RANK + B_ROPE] = qr[:, h * B_ROPE:(h + 1) * B_ROPE]
            qc_s[h * tq:(h + 1) * tq, MLA_RANK + B_ROPE:] = jnp.zeros((tq, MLA_KC - MLA_RANK - B_ROPE), BF16)
        m_s[...] = jnp.full_like(m_s, NEG)
        l_s[...] = jnp.zeros_like(l_s)
        acc_s[...] = jnp.zeros_like(acc_s)

    def update(masked, nk):
        kc = kc_ref[0, 0:nk, :]
        v = kc[:, :MLA_RANK]

        def scores(h):
            return _dot_nt(qc_s[h * tq:(h + 1) * tq, :], kc)

        pending = [scores(h) for h in range(ATTN_LOOKAHEAD)]
        for h in range(B_HEADS):
            rows = slice(h * tq, (h + 1) * tq)
            s = pending.pop(0)
            if h + ATTN_LOOKAHEAD < B_HEADS:
                pending.append(scores(h + ATTN_LOOKAHEAD))
            if masked:
                ri = lax.broadcasted_iota(jnp.int32, s.shape, 0)
                ci = lax.broadcasted_iota(jnp.int32, s.shape, 1)
                s = jnp.where(ri - ci >= kj * tk - qi * tq, s, NEG)
            m_prev = m_s[rows, :]
            m_new = jnp.maximum(m_prev, jnp.max(s, -1, keepdims=True))
            alpha = jnp.exp2(m_prev - m_new)
            p = jnp.exp2(s - m_new)
            l_s[rows, :] = alpha * l_s[rows, :] + jnp.sum(p, -1, keepdims=True)
            acc_s[rows, :] = alpha * acc_s[rows, :] + _dot(p.astype(BF16), v)
            m_s[rows, :] = m_new

    visible = (qi + 1) * tq - kj * tk

    @pl.when(visible > tk)
    def _plain():
        update(False, tk)

    for nk in range(tq, tk + tq, tq):
        @pl.when(visible == nk)
        def _masked(nk=nk):
            update(True, nk)

    @pl.when(last_ref[step] == 1)
    def _fin():
        o = acc_s[...] / l_s[...]
        ys = [_dot(o[h * tq:(h + 1) * tq].astype(BF16), wkv_ref[h]) for h in range(B_HEADS)]
        yb_ref[0] = jnp.concatenate(ys, axis=1).astype(yb_ref.dtype)


def _mla_attn(qnope, qrope, kcat, wkn, wkv, tq, tk):
    nb, t, _ = qnope.shape
    qi, kj, last = _causal_pairs(t, tq, tk)
    grid_spec = pltpu.PrefetchScalarGridSpec(
        num_scalar_prefetch=3, grid=(nb, len(qi)),
        in_specs=[pl.BlockSpec((1, tq, qnope.shape[-1]), lambda b, s, qi, kj, la: (b, qi[s], 0)),
                  pl.BlockSpec((1, tq, qrope.shape[-1]), lambda b, s, qi, kj, la: (b, qi[s], 0)),
                  pl.BlockSpec((1, tk, MLA_KC), lambda b, s, qi, kj, la: (b, kj[s], 0)),
                  _full_spec(wkn.shape), _full_spec(wkv.shape)],
        out_specs=pl.BlockSpec((1, tq, B_HEADS * B_V), lambda b, s, qi, kj, la: (b, qi[s], 0)),
        scratch_shapes=[pltpu.VMEM((B_HEADS * tq, MLA_KC), BF16),
                        pltpu.VMEM((B_HEADS * tq, 1), F32), pltpu.VMEM((B_HEADS * tq, 1), F32),
                        pltpu.VMEM((B_HEADS * tq, MLA_RANK), F32)])
    return pl.pallas_call(
        functools.partial(_mla_attn_kernel, tq=tq, tk=tk), grid_spec=grid_spec,
        out_shape=jax.ShapeDtypeStruct((nb, t, B_HEADS * B_V), BF16),
        compiler_params=_cp("arbitrary", "arbitrary"), name="mla_attn")(
            jnp.asarray(qi), jnp.asarray(kj), jnp.asarray(last), qnope, qrope, kcat, wkn, wkv)


CONV_ROWS = 32


def _conv_prompt(u, halo_ref, ext_s, w_ref, i, k):
    tm = u.shape[0]
    hr = halo_ref.shape[1]
    ext_s[0, 0:hr] = jnp.where(i > 0, halo_ref[0], 0.0)
    ext_s[0, hr:hr + tm] = u
    n = hr + tm - SUBLANE
    for p in range(1, SUBLANE):
        ext_s[p, 0:n] = ext_s[0, p:p + n]
    off = hr - (k - 1)
    blocks = []
    for r in range(0, tm, CONV_ROWS):
        acc = w_ref[k - 1:k] * ext_s[0, hr + r:hr + r + CONV_ROWS]
        for j in range(k - 1):
            p = (off + j) % SUBLANE
            base = off + j + r - p
            acc = acc + w_ref[j:j + 1] * ext_s[p, base:base + CONV_ROWS]
        blocks.append(acc)
    return jnp.concatenate(blocks, axis=0)


def _conv_sample(u, hist_ref, w_ref, k):
    acc = w_ref[k - 1:k] * u
    for j in range(k - 1):
        acc = acc + w_ref[j:j + 1] * hist_ref[j]
    return acc


def _halo_spec(tm, hr, c):
    return pl.BlockSpec((1, hr, c), lambda b, i: (b, jnp.maximum(i * (tm // hr) - 1, 0), 0))


def _even_out_kernel(*refs, alpha, sample):
    if sample:
        (u_ref, hist_ref, gb_ref, yb_ref, x_ref, mod_ref, cw_ref, wa_ref, wb_ref, g_ref, b_ref, o_ref) = refs
        conv = _conv_sample(u_ref[0], hist_ref, cw_ref, A_CONV)
    else:
        (u_ref, halo_ref, gb_ref, yb_ref, x_ref, mod_ref, cw_ref, wa_ref, wb_ref, g_ref, b_ref, o_ref, ext_s) = refs
        conv = _conv_prompt(u_ref[0], halo_ref, ext_s, cw_ref, pl.program_id(1), A_CONV)
    ya = gb_ref[0] * conv
    mix = _dot(ya.astype(BF16), wa_ref[...]) + _dot(yb_ref[0], wb_ref[...])
    m = mod_ref[0]
    o_ref[0] = _ln(alpha * x_ref[0] + (1.0 + m[2]) * mix, g_ref[...], b_ref[...])


def _mixer_out(kernel, u, side, others, other_specs, x, mod, tail, tm, alpha, sample, k):
    nb, t, d = x.shape
    c = u.shape[-1]
    r = mod.shape[2]
    hr = 8 if k - 1 <= 8 else 32
    if sample:
        side_arg, side_spec, scratch = side, _full_spec(side.shape), []
    else:
        side_arg, side_spec, scratch = u, _halo_spec(tm, hr, c), [pltpu.VMEM((SUBLANE, tm + hr, c), F32)]
    return pl.pallas_call(
        functools.partial(kernel, alpha=alpha, sample=sample), grid=(nb, t // tm),
        in_specs=[_row_spec(tm, c), side_spec] + list(other_specs)
        + [_row_spec(tm, d), _mod_spec(r, d)] + [_full_spec(w.shape) for w in tail],
        out_specs=_row_spec(tm, d), out_shape=jax.ShapeDtypeStruct((nb, t, d), F32),
        scratch_shapes=scratch, compiler_params=_cp("arbitrary", "arbitrary"),
        name=kernel.__name__.strip("_"))(u, side_arg, *others, x, mod, *tail)


def _ffn_kernel(x_ref, mod_ref, w1_ref, w2_ref, g_ref, b_ref, o_ref, h_s, acc_s, *, alpha):
    f = pl.program_id(2)

    @pl.when(f == 0)
    def _first():
        m = mod_ref[0]
        h_s[...] = (x_ref[0] * (1.0 + m[4]) + m[3]).astype(BF16)
        acc_s[...] = jnp.zeros_like(acc_s)

    a = jnp.square(jnp.maximum(_dot(h_s[...], w1_ref[...]), 0.0))
    acc_s[...] += _dot(a.astype(BF16), w2_ref[...])

    @pl.when(f == pl.num_programs(2) - 1)
    def _last():
        m = mod_ref[0]
        o_ref[0] = _ln(alpha * x_ref[0] + (1.0 + m[5]) * acc_s[...], g_ref[...], b_ref[...])


def _ffn(x, mod, w1, w2, g, b, tm, tf, alpha):
    nb, t, d = x.shape
    r = mod.shape[2]
    ff = w1.shape[1]
    return pl.pallas_call(
        functools.partial(_ffn_kernel, alpha=alpha), grid=(nb, t // tm, ff // tf),
        in_specs=[_row_spec(tm, d), _mod_spec(r, d),
                  pl.BlockSpec((d, tf), lambda b, i, f: (0, f)), pl.BlockSpec((tf, d), lambda b, i, f: (f, 0)),
                  _full_spec(g.shape), _full_spec(b.shape)],
        out_specs=_row_spec(tm, d), out_shape=jax.ShapeDtypeStruct((nb, t, d), F32),
        scratch_shapes=[pltpu.VMEM((tm, d), BF16), pltpu.VMEM((tm, d), F32)],
        compiler_params=_cp("arbitrary", "arbitrary", "arbitrary"), name="ffn")(x, mod, w1, w2, g, b)


O_GA, O_GB, O_Q, O_K, O_V, O_F, O_END = 0, 512, 1024, 1536, 2048, 2560, 2568
O_COLS = 2688


def _log_sigmoid(x):
    return jnp.minimum(x, 0.0) - jnp.log(1.0 + jnp.exp(-jnp.abs(x)))


FOX_KC = 128
FOX_DW = O_K - O_Q


def _fox_key_placement():
    p = np.zeros((FOX_DW + 3 * LANE, D_HEADS * FOX_KC), np.float32)
    for h in range(D_HEADS):
        for c in range(D_HEAD_DIM):
            p[h * D_HEAD_DIM + c, h * FOX_KC + c] = 1.0
        for piece in range(3):
            p[FOX_DW + piece * LANE + h, h * FOX_KC + D_HEAD_DIM + piece] = 1.0
    return jnp.asarray(p, BF16)


def _odd_in_kernel(*refs, prompt):
    if prompt:
        (x_ref, mod_ref, w_ref, bf_ref, place_ref, u_ref, kt_ref, vt_ref, lf_ref, ka_ref, qt_ref, vtb_ref,
         carry_s) = refs
    else:
        (x_ref, mod_ref, w_ref, bf_ref, u_ref, q_ref, k_ref, v_ref, lf_ref) = refs
    m = mod_ref[0]
    h = x_ref[0] * (1.0 + m[1]) + m[0]
    z = _dot(h.astype(BF16), w_ref[...])
    tm = z.shape[0]
    u_ref[0] = z[:, O_GA:O_GB] * jax.nn.sigmoid(z[:, O_GB:O_Q])
    q = z[:, O_Q:O_K] * FOX_SCALE
    k = z[:, O_K:O_V]
    v = z[:, O_V:O_F]
    lane = lax.broadcasted_iota(jnp.int32, (tm, O_COLS - O_F), 1)
    lf = jnp.where(lane < D_HEADS, _log_sigmoid(z[:, O_F:O_COLS] + bf_ref[...]), 0.0)
    lf_ref[0] = lf[:, :D_HEADS]
    if not prompt:
        q_ref[0] = q.astype(BF16)
        k_ref[0] = k
        v_ref[0] = v
        return

    kt_ref[0] = k.T
    vt = v.T
    vt_ref[0] = vt
    vtb_ref[0] = vt.astype(BF16)
    qt_ref[0] = q.T.astype(BF16)

    @pl.when(pl.program_id(1) == 0)
    def _reset():
        carry_s[...] = jnp.zeros_like(carry_s)

    tri = (lax.broadcasted_iota(jnp.int32, (tm, tm), 0) >= lax.broadcasted_iota(jnp.int32, (tm, tm), 1))
    tri = jnp.where(tri, 1.0, 0.0).astype(BF16)
    cum = sum(_dot(tri, piece) for piece in _split3(lf)) + carry_s[...]
    carry_s[...] = cum[tm - 1:tm]
    cat = jnp.concatenate((k.astype(BF16),) + _split3(cum * LOG2E), axis=1)
    ka_ref[0] = _dot(cat, place_ref[...]).astype(BF16)


def _odd_in(x, mod, w_in, bf, tm, prompt):
    nb, t, d = x.shape
    r = mod.shape[2]
    cw = O_GB - O_GA
    ins = [x, mod, w_in, bf]
    in_specs = [_row_spec(tm, d), _mod_spec(r, d), _full_spec(w_in.shape), _full_spec(bf.shape)]
    if prompt:
        place = _fox_key_placement()
        ins.append(place)
        in_specs.append(_full_spec(place.shape))
        rows = [(cw, F32)]
        cols = [(FOX_DW, F32), (FOX_DW, F32)]
        rows2 = [(D_HEADS, F32), (D_HEADS * FOX_KC, BF16)]
        cols2 = [(FOX_DW, BF16), (FOX_DW, BF16)]
        out_specs = ([_row_spec(tm, c) for c, _ in rows] + [_time_minor_spec(c, tm) for c, _ in cols]
                     + [_row_spec(tm, c) for c, _ in rows2] + [_time_minor_spec(c, tm) for c, _ in cols2])
        out_shape = ([jax.ShapeDtypeStruct((nb, t, c), dt) for c, dt in rows]
                     + [jax.ShapeDtypeStruct((nb, c, t), dt) for c, dt in cols]
                     + [jax.ShapeDtypeStruct((nb, t, c), dt) for c, dt in rows2]
                     + [jax.ShapeDtypeStruct((nb, c, t), dt) for c, dt in cols2])
        scratch = [pltpu.VMEM((1, O_COLS - O_F), F32)]
    else:
        outs = [(cw, F32), (FOX_DW, BF16), (FOX_DW, F32), (FOX_DW, F32), (D_HEADS, F32)]
        out_specs = [_row_spec(tm, c) for c, _ in outs]
        out_shape = [jax.ShapeDtypeStruct((nb, t, c), dt) for c, dt in outs]
        scratch = []
    return pl.pallas_call(
        functools.partial(_odd_in_kernel, prompt=prompt), grid=(nb, t // tm),
        in_specs=in_specs, out_specs=out_specs, out_shape=out_shape,
        scratch_shapes=scratch, compiler_params=_cp("arbitrary", "arbitrary"), name="odd_in")(*ins)


def _fox_attn_kernel(qi_ref, kj_ref, last_ref, ka_ref, qt_ref, vt_ref, o_ref, qa_s, m_s, l_s, acc_s, *, tq, tk):
    step = pl.program_id(1)
    qi = qi_ref[step]
    kj = kj_ref[step]
    hd = D_HEAD_DIM

    @pl.when(kj == 0)
    def _init():
        minus_ones = jnp.where(lax.broadcasted_iota(jnp.int32, (FOX_KC - hd, tq), 0) < 3, -1.0, 0.0).astype(BF16)
        for h in range(D_HEADS):
            qa_s[h, 0:hd, :] = qt_ref[0, h * hd:(h + 1) * hd, :]
            qa_s[h, hd:FOX_KC, :] = minus_ones
        m_s[...] = jnp.full_like(m_s, NEG)
        l_s[...] = jnp.zeros_like(l_s)
        acc_s[...] = jnp.zeros_like(acc_s)

    def update(masked):
        def scores(h):
            return _dot(ka_ref[0, :, h * FOX_KC:(h + 1) * FOX_KC], qa_s[h])

        pending = [scores(h) for h in range(ATTN_LOOKAHEAD)]
        for h in range(D_HEADS):
            s = pending.pop(0)
            if h + ATTN_LOOKAHEAD < D_HEADS:
                pending.append(scores(h + ATTN_LOOKAHEAD))
            if masked:
                ki = lax.broadcasted_iota(jnp.int32, s.shape, 0)
                qq = lax.broadcasted_iota(jnp.int32, s.shape, 1)
                s = jnp.where(qq - ki >= kj * tk - qi * tq, s, NEG)
            m_prev = m_s[h]
            m_new = jnp.maximum(m_prev, jnp.max(s, 0, keepdims=True))
            alpha = jnp.exp2(m_prev - m_new)
            p = jnp.exp2(s - m_new)
            l_s[h] = alpha * l_s[h] + jnp.sum(p, 0, keepdims=True)
            acc_s[h] = alpha * acc_s[h] + _dot(vt_ref[0, h * hd:(h + 1) * hd, :], p.astype(BF16))
            m_s[h] = m_new

    needs_mask = (kj + 1) * tk - 1 > qi * tq

    @pl.when(needs_mask)
    def _masked():
        update(True)

    @pl.when(jnp.logical_not(needs_mask))
    def _plain():
        update(False)

    @pl.when(last_ref[step] == 1)
    def _fin():
        for h in range(D_HEADS):
            o_ref[0, h * hd:(h + 1) * hd, :] = (acc_s[h] / l_s[h]).astype(o_ref.dtype)


def _fox_attn(ka, q_t, v_t, tq, tk):
    nb, t, _ = ka.shape
    dw = v_t.shape[1]
    qi, kj, last = _causal_pairs(t, tq, tk)
    qmap = lambda b, s, qi, kj, la: (b, 0, qi[s])
    grid_spec = pltpu.PrefetchScalarGridSpec(
        num_scalar_prefetch=3, grid=(nb, len(qi)),
        in_specs=[pl.BlockSpec((1, tk, ka.shape[-1]), lambda b, s, qi, kj, la: (b, kj[s], 0)),
                  pl.BlockSpec((1, dw, tq), qmap),
                  pl.BlockSpec((1, dw, tk), lambda b, s, qi, kj, la: (b, 0, kj[s]))],
        out_specs=pl.BlockSpec((1, dw, tq), qmap),
        scratch_shapes=[pltpu.VMEM((D_HEADS, FOX_KC, tq), BF16), pltpu.VMEM((D_HEADS, 1, tq), F32),
                        pltpu.VMEM((D_HEADS, 1, tq), F32), pltpu.VMEM((D_HEADS, D_HEAD_DIM, tq), F32)])
    return pl.pallas_call(
        functools.partial(_fox_attn_kernel, tq=tq, tk=tk), grid_spec=grid_spec,
        out_shape=jax.ShapeDtypeStruct((nb, dw, t), BF16),
        compiler_params=_cp("arbitrary", "arbitrary"), name="fox_attn")(
            jnp.asarray(qi), jnp.asarray(kj), jnp.asarray(last), ka, q_t, v_t)


def _odd_out_kernel(*refs, alpha, sample):
    if sample:
        (u_ref, hist_ref, o_in_ref, x_ref, mod_ref, cw_ref, cb_ref, cg_ref, cbb_ref, wa_ref, wb_ref, g_ref, b_ref,
         o_ref) = refs
        conv = _conv_sample(u_ref[0], hist_ref, cw_ref, C_CONV)
    else:
        (u_ref, halo_ref, o_in_ref, x_ref, mod_ref, cw_ref, cb_ref, cg_ref, cbb_ref, wa_ref, wb_ref, g_ref, b_ref,
         o_ref, ext_s) = refs
        conv = _conv_prompt(u_ref[0], halo_ref, ext_s, cw_ref, pl.program_id(1), C_CONV)
    y = _ln(conv + cb_ref[...], cg_ref[...], cbb_ref[...])
    yc = y * jax.nn.sigmoid(y)
    attn = lax.dot_general(o_in_ref[0], wb_ref[...], (((0,), (0,)), ((), ())), preferred_element_type=F32)
    mix = _dot(yc.astype(BF16), wa_ref[...]) + attn
    m = mod_ref[0]
    o_ref[0] = _ln(alpha * x_ref[0] + (1.0 + m[2]) * mix, g_ref[...], b_ref[...])


def _qlat_kernel(qn_ref, wkn_ref, o_ref):
    qn = qn_ref[...]
    for h in range(B_HEADS):
        o_ref[h] = (_dot(qn[:, h * B_NOPE:(h + 1) * B_NOPE], wkn_ref[h]) * MLA_SCALE).astype(BF16)


def _qlat(qnope, wkn):
    n = qnope.shape[0]
    return pl.pallas_call(_qlat_kernel, out_shape=jax.ShapeDtypeStruct((B_HEADS, n, wkn.shape[-1]), BF16),
                          name="qlat")(qnope, wkn)


def _yb_kernel(o_ref, wkv_ref, y_ref):
    ys = [_dot(o_ref[h].astype(BF16), wkv_ref[h]) for h in range(B_HEADS)]
    y_ref[...] = jnp.concatenate(ys, axis=1).astype(y_ref.dtype)


def _yb(o_lat_t, wkv):
    n = o_lat_t.shape[1]
    return pl.pallas_call(_yb_kernel, out_shape=jax.ShapeDtypeStruct((n, B_HEADS * B_V), BF16),
                          name="yb")(o_lat_t, wkv)


def _online_update(s, m_s, l_s, acc_s, values, values_transposed):
    m_prev = m_s[...]
    m_new = jnp.maximum(m_prev, jnp.max(s, -1, keepdims=True))
    alpha = jnp.exp2(m_prev - m_new)
    p = jnp.exp2(s - m_new)
    l_s[...] = alpha * l_s[...] + jnp.sum(p, -1, keepdims=True)
    pv = _dot_nt(p.astype(BF16), values) if values_transposed else _dot(p.astype(BF16), values)
    acc_s[...] = alpha * acc_s[...] + pv
    m_s[...] = m_new


def _mla_dec_kernel(pt_ref, ql_ref, qr_ref, cn_ref, kn_ref, *rest, npg, page):
    lat_refs, kr_refs = rest[:npg], rest[npg:2 * npg]
    o_ref, lat_s, kr_s, m_s, l_s, acc_s = rest[2 * npg:]
    c = pl.program_id(1)

    @pl.when(c == 0)
    def _init():
        m_s[...] = jnp.full_like(m_s, NEG)
        l_s[...] = jnp.zeros_like(l_s)
        acc_s[...] = jnp.zeros_like(acc_s)

    for i in range(npg):
        lat_s[i * page:(i + 1) * page, :] = lat_refs[i][0, 0].astype(BF16)
        kr_s[:, i * page:(i + 1) * page] = kr_refs[i][0, 0].astype(BF16)
    ql = ql_ref[0]
    qr = qr_ref[0]
    s = _dot_nt(ql, lat_s[...]) + _dot(qr, kr_s[...])
    _online_update(s, m_s, l_s, acc_s, lat_s[...], False)

    @pl.when(c == pl.num_programs(1) - 1)
    def _fin():
        cn = cn_ref[0].astype(BF16).astype(F32)
        kn = kn_ref[0].astype(BF16).astype(F32)
        s_new = (jnp.sum(ql.astype(F32) * cn, -1, keepdims=True) + jnp.sum(qr.astype(F32) * kn, -1, keepdims=True))
        m_prev = m_s[...]
        m_new = jnp.maximum(m_prev, s_new)
        alpha = jnp.exp2(m_prev - m_new)
        p_new = jnp.exp2(s_new - m_new)
        o_ref[0] = (alpha * acc_s[...] + p_new * cn) / (alpha * l_s[...] + p_new)


def _page_map(layer, n_pages, npg, i):
    return lambda b, c, pt: (layer, pt[b * n_pages + c * npg + i], 0, 0)


def _mla_dec(pt_flat, ql, qr, cn, kn, lat_pool, krt_pool, layer, n_pages, npg):
    ns = ql.shape[0]
    page, rank = lat_pool.shape[2], lat_pool.shape[3]
    seq = lambda shape: pl.BlockSpec((1,) + shape, lambda b, c, pt: (b, 0, 0))
    grid_spec = pltpu.PrefetchScalarGridSpec(
        num_scalar_prefetch=1, grid=(ns, n_pages // npg),
        in_specs=[seq((B_HEADS, rank)), seq((B_HEADS, B_ROPE)), seq((1, rank)), seq((1, B_ROPE))]
        + [pl.BlockSpec((1, 1, page, rank), _page_map(layer, n_pages, npg, i)) for i in range(npg)]
        + [pl.BlockSpec((1, 1, B_ROPE, page), _page_map(layer, n_pages, npg, i)) for i in range(npg)],
        out_specs=seq((B_HEADS, rank)),
        scratch_shapes=[pltpu.VMEM((npg * page, rank), BF16), pltpu.VMEM((B_ROPE, npg * page), BF16),
                        pltpu.VMEM((B_HEADS, 1), F32), pltpu.VMEM((B_HEADS, 1), F32),
                        pltpu.VMEM((B_HEADS, rank), F32)])
    return pl.pallas_call(
        functools.partial(_mla_dec_kernel, npg=npg, page=page), grid_spec=grid_spec,
        out_shape=jax.ShapeDtypeStruct((ns, B_HEADS, rank), F32),
        compiler_params=_cp("arbitrary", "arbitrary"), name="mla_dec")(
            pt_flat, ql, qr, cn, kn, *([lat_pool] * npg), *([krt_pool] * npg))


def _fox_dec_kernel(pt_ref, q_ref, kn_ref, vn_ref, lfn_ref, *rest, npg, page):
    k_refs, v_refs, lf_refs = rest[:npg], rest[npg:2 * npg], rest[2 * npg:3 * npg]
    o_ref, k_s, v_s, m_s, l_s, acc_s, carry_s = rest[3 * npg:]
    c = pl.program_id(1)
    nh = D_HEADS
    width = nh * D_HEAD_DIM

    @pl.when(c == 0)
    def _init():
        m_s[...] = jnp.full_like(m_s, NEG)
        l_s[...] = jnp.zeros_like(l_s)
        acc_s[...] = jnp.zeros_like(acc_s)
        carry_s[...] = jnp.zeros_like(carry_s)

    for i in range(npg):
        k_s[:, i * page:(i + 1) * page] = k_refs[i][0, 0].astype(BF16)
        v_s[:, i * page:(i + 1) * page] = v_refs[i][0, 0].astype(BF16)

    diag = (lax.broadcasted_iota(jnp.int32, (nh, width), 1) // D_HEAD_DIM
            == lax.broadcasted_iota(jnp.int32, (nh, width), 0))
    qbd = jnp.where(diag, q_ref[0].astype(F32), 0.0)

    rows = nh * npg
    y = jnp.concatenate([lf_refs[i][0, 0] for i in range(npg)], axis=0)
    upper = (lax.broadcasted_iota(jnp.int32, (page, page), 0) <= lax.broadcasted_iota(jnp.int32, (page, page), 1))
    upper = jnp.where(upper, 1.0, 0.0).astype(BF16)
    z = sum(_dot(piece, upper) for piece in _split3(y))
    r0 = lax.broadcasted_iota(jnp.int32, (rows, rows), 0)
    r1 = lax.broadcasted_iota(jnp.int32, (rows, rows), 1)
    earlier = jnp.where(((r0 % nh) == (r1 % nh)) & (r1 // nh < r0 // nh), 1.0, 0.0).astype(BF16)
    totals = jnp.broadcast_to(z[:, page - 1:page], (rows, page))
    off = sum(_dot(earlier, piece) for piece in _split3(totals))
    cum = z + off + jnp.concatenate([carry_s[...]] * npg, axis=0)
    carry_s[...] = cum[rows - nh:rows, page - 1:page]
    bias = jnp.concatenate([cum[nh * i:nh * (i + 1), :] for i in range(npg)], axis=1)

    s = _dot(qbd.astype(BF16), k_s[...]) - bias * LOG2E
    _online_update(s, m_s, l_s, acc_s, v_s[...], True)

    @pl.when(c == pl.num_programs(1) - 1)
    def _fin():
        kn = kn_ref[0].astype(BF16).astype(F32)
        vn = vn_ref[0].astype(BF16).astype(F32)
        s_new = jnp.sum(qbd * kn, -1, keepdims=True) - (carry_s[...] + lfn_ref[0]) * LOG2E
        m_prev = m_s[...]
        m_new = jnp.maximum(m_prev, s_new)
        alpha = jnp.exp2(m_prev - m_new)
        p_new = jnp.exp2(s_new - m_new)
        o = (alpha * acc_s[...] + p_new * vn) / (alpha * l_s[...] + p_new)
        o_ref[0] = jnp.sum(jnp.where(diag, o, 0.0), axis=0, keepdims=True)


def _fox_dec(pt_flat, q, kn, vn, lfn, kt_pool, vt_pool, lft_pool, layer, n_pages, npg):
    ns = q.shape[0]
    width, page = kt_pool.shape[2], kt_pool.shape[3]
    seq = lambda shape: pl.BlockSpec((1,) + shape, lambda b, c, pt: (b, 0, 0))
    pages = lambda shape: [pl.BlockSpec((1, 1) + shape, _page_map(layer, n_pages, npg, i)) for i in range(npg)]
    grid_spec = pltpu.PrefetchScalarGridSpec(
        num_scalar_prefetch=1, grid=(ns, n_pages // npg),
        in_specs=[seq((1, width)), seq((1, width)), seq((1, width)), seq((D_HEADS, 1))]
        + pages((width, page)) + pages((width, page)) + pages((D_HEADS, page)),
        out_specs=seq((1, width)),
        scratch_shapes=[pltpu.VMEM((width, npg * page), BF16), pltpu.VMEM((width, npg * page), BF16),
                        pltpu.VMEM((D_HEADS, 1), F32), pltpu.VMEM((D_HEADS, 1), F32),
                        pltpu.VMEM((D_HEADS, width), F32), pltpu.VMEM((D_HEADS, 1), F32)])
    return pl.pallas_call(
        functools.partial(_fox_dec_kernel, npg=npg, page=page), grid_spec=grid_spec,
        out_shape=jax.ShapeDtypeStruct((ns, 1, width), F32),
        compiler_params=_cp("arbitrary", "arbitrary"), name="fox_dec")(
            pt_flat, q, kn, vn, lfn, *([kt_pool] * npg), *([vt_pool] * npg), *([lft_pool] * npg))


def _rope_tables(pos):
    half = B_ROPE // 2
    inv = ROPE_THETA ** (-jnp.arange(half, dtype=F32) / half)
    ang = pos.astype(F32)[:, None] * inv
    cos = jnp.concatenate([jnp.cos(ang)] * 2, axis=-1)
    sin = jnp.concatenate([jnp.sin(ang)] * 2, axis=-1)
    return cos, sin, jnp.tile(cos, (1, B_HEADS)), jnp.tile(sin, (1, B_HEADS))


def _rot_cols(w):
    half = w.shape[-1] // 2
    return jnp.concatenate([-w[..., half:], w[..., :half]], axis=-1)


def _prep_even(w_in, w_q_b, w_kv_b):
    d = w_in.shape[0]
    kr = w_in[:, E_KR:E_KRR]
    w_in_p = jnp.concatenate([w_in, _rot_cols(kr), jnp.zeros((d, E_COLS - E_END), w_in.dtype)], axis=1)
    qr = w_q_b.shape[0]
    wq = w_q_b.reshape(qr, B_HEADS, B_NOPE + B_ROPE)
    rp = wq[:, :, B_NOPE:]
    wq_p = jnp.concatenate([wq[:, :, :B_NOPE].reshape(qr, -1), rp.reshape(qr, -1), _rot_cols(rp).reshape(qr, -1)],
                           axis=1)
    wkn = jnp.transpose(w_kv_b[:, :, :B_NOPE], (1, 2, 0))
    wkv = jnp.transpose(w_kv_b[:, :, B_NOPE:], (1, 0, 2))
    return w_in_p.astype(BF16), wq_p.astype(BF16), wkn.astype(BF16), wkv.astype(BF16)


def kernel(x_prompt, x_sample, cache_mla_latent, cache_mla_krope, state_conv_a, cache_fox_k, cache_fox_v,
           cache_fox_logf, state_conv_c, page_table, c_prompt, c_sample, w_ada, b_ada, ln_g, ln_b,
           w_in_even, w_q_b, w_kv_b, q_norm, kv_norm, conv_a, w_out_even, w_in_odd, b_f, conv_c,
           conv_c_b, cn_g, cn_b, w_out_odd, w_ff1, w_ff2):
    depth = w_ada.shape[0]
    alpha = float((2 * depth) ** 0.25)
    nbp, seq, d = x_prompt.shape
    ns = x_sample.shape[0]
    n_pages = page_table.shape[1]
    page = cache_mla_latent.shape[2]
    past_len = n_pages * page
    n_pool = cache_fox_k.shape[1]
    aw = conv_a.shape[-1]
    cw = conv_c.shape[-1]
    dw = D_HEADS * D_HEAD_DIM

    tm = min(256, seq)
    tq_mla, tk_mla = min(256, seq), min(1024, seq)
    tq_fox, tk_fox = min(512, seq), min(512, seq)
    tm_ffn, tf = min(1024, seq), min(1024, w_ff1.shape[-1])
    npg = min(32, n_pages)

    n_c = nbp + ns
    n_pad = -n_c % 8
    c_all = jnp.concatenate([c_prompt, c_sample, jnp.zeros((n_pad, d), F32)], axis=0)
    mods = _ada(c_all, w_ada, b_ada)
    mod_p = jnp.transpose(mods[:, :, :nbp], (0, 2, 1, 3))[:, :, :, None, :]
    mod_s = mods[:, None, :, nbp:n_c]

    tabs_p = _rope_tables(jnp.arange(seq, dtype=jnp.int32))
    tabs_s = _rope_tables(jnp.full((ns,), past_len, dtype=jnp.int32))
    pt_flat = page_table.reshape(-1)
    keys_minor = lambda a: jnp.transpose(a, (0, 1, 3, 4, 2)).reshape(a.shape[:2] + (-1, a.shape[2]))
    kt_pool = keys_minor(cache_fox_k)
    vt_pool = keys_minor(cache_fox_v)
    lft_pool = jnp.swapaxes(cache_fox_logf, 2, 3)
    krt_pool = jnp.swapaxes(cache_mla_krope, 2, 3)

    xp = x_prompt
    xs = x_sample.reshape(1, ns, d)
    even_p, even_s, odd_p, odd_s = [], [], [], []
    for l in range(depth):
        i = l // 2
        row = lambda a: a.reshape(1, -1)
        g0, b0, g1, b1 = row(ln_g[l, 0]), row(ln_b[l, 0]), row(ln_g[l, 1]), row(ln_b[l, 1])
        if l % 2 == 0:
            w_in_p, wq_p, wkn, wkv = _prep_even(w_in_even[i], w_q_b[i], w_kv_b[i])
            w_out = w_out_even[i].astype(BF16)
            tail = (conv_a[i], w_out[:aw], w_out[aw:], g0, b0)
            qn, kvn = row(q_norm[i]), row(kv_norm[i])
            u, gb, ckv, kr, kcat, qnope, qrope = _even_in(xp, mod_p[l], w_in_p, qn, kvn, wq_p, *tabs_p, tm)
            yb = _mla_attn(qnope, qrope, kcat, wkn, wkv, tq_mla, tk_mla)
            xp = _mixer_out(_even_out_kernel, u, None, (gb, yb), (_row_spec(tm, aw), _row_spec(tm, yb.shape[-1])),
                            xp, mod_p[l], tail, tm, alpha, False, A_CONV)
            even_p.append((ckv, kr, u[:, seq - (A_CONV - 1):]))
            u, gb, ckv, kr, _, qnope, qrope = _even_in(xs, mod_s[l], w_in_p, qn, kvn, wq_p, *tabs_s, ns)
            ql = jnp.transpose(_qlat(qnope[0], wkn), (1, 0, 2))
            qr = qrope[0].reshape(ns, B_HEADS, B_ROPE)
            o_lat = _mla_dec(pt_flat, ql, qr, ckv[0][:, None], kr[0][:, None], cache_mla_latent, krt_pool,
                             i, n_pages, npg)
            yb = _yb(jnp.transpose(o_lat, (1, 0, 2)), wkv)[None]
            hist = jnp.transpose(state_conv_a[i], (1, 0, 2))
            xs = _mixer_out(_even_out_kernel, u, hist, (gb, yb), (_row_spec(ns, aw), _row_spec(ns, yb.shape[-1])),
                            xs, mod_s[l], tail, ns, alpha, True, A_CONV)
            even_s.append((ckv[0][:, None], kr[0][:, None],
                           jnp.concatenate([state_conv_a[i][:, 1:], u[0][:, None]], axis=1)))
        else:
            w_in_p = jnp.concatenate([w_in_odd[i], jnp.zeros((d, O_COLS - O_END), F32)], axis=1).astype(BF16)
            w_out = w_out_odd[i].astype(BF16)
            tail = (conv_c[i], row(conv_c_b[i]), row(cn_g[i]), row(cn_b[i]), w_out[:cw], w_out[cw:], g0, b0)
            bf = jnp.concatenate([b_f[i], jnp.zeros((O_COLS - O_F - D_HEADS,), F32)]).reshape(1, -1)
            u, k_t, v_t, lf, ka, q_tb, v_tb = _odd_in(xp, mod_p[l], w_in_p, bf, tm, True)
            o_t = _fox_attn(ka, q_tb, v_tb, tq_fox, tk_fox)
            xp = _mixer_out(_odd_out_kernel, u, None, (o_t,), (_time_minor_spec(dw, tm),),
                            xp, mod_p[l], tail, tm, alpha, False, C_CONV)
            state = lambda a: jnp.transpose(a.reshape(nbp, D_HEADS, D_HEAD_DIM, seq), (0, 3, 1, 2))
            odd_p.append((state(k_t), state(v_t), lf, u[:, seq - (C_CONV - 1):]))
            u, q, k, v, lf = _odd_in(xs, mod_s[l], w_in_p, bf, ns, False)
            o = _fox_dec(pt_flat, q[0][:, None], k[0][:, None], v[0][:, None], lf[0][:, :, None],
                         kt_pool, vt_pool, lft_pool, i, n_pages, npg)
            o_t = jnp.transpose(o.reshape(ns, dw)).astype(BF16)[None]
            hist = jnp.transpose(state_conv_c[i], (1, 0, 2))
            xs = _mixer_out(_odd_out_kernel, u, hist, (o_t,), (_time_minor_spec(dw, ns),),
                            xs, mod_s[l], tail, ns, alpha, True, C_CONV)
            hd = (ns, 1, D_HEADS, D_HEAD_DIM)
            odd_s.append((k[0].reshape(hd), v[0].reshape(hd), lf[0][:, None],
                          jnp.concatenate([state_conv_c[i][:, 1:], u[0][:, None]], axis=1)))
        w1, w2 = w_ff1[l].astype(BF16), w_ff2[l].astype(BF16)
        xp = _ffn(xp, mod_p[l], w1, w2, g1, b1, tm_ffn, tf, alpha)
        xs = _ffn(xs, mod_s[l], w1, w2, g1, b1, ns, tf, alpha)

    stack = lambda states: [jnp.stack(s) for s in zip(*states)]
    lat_p, kr_p, ca_p = stack(even_p)
    k_p, v_p, lf_p, cc_p = stack(odd_p)
    lat_s, kr_s, ca_s = stack(even_s)
    k_s, v_s, lf_s, cc_s = stack(odd_s)
    return (xp, xs.reshape(ns, 1, d), lat_p, kr_p, ca_p, k_p, v_p, lf_p, cc_p,
            lat_s, kr_s, ca_s, k_s, v_s, lf_s, cc_s)
```

```python
import functools

import numpy as np
import jax
import jax.numpy as jnp
from jax import lax
from jax.experimental import pallas as pl
from jax.experimental.pallas import tpu as pltpu

F32 = jnp.float32
BF16 = jnp.bfloat16

A_CONV = 3
C_CONV = 31
B_HEADS = 8
B_NOPE = 64
B_ROPE = 32
B_V = 64
D_HEADS = 8
D_HEAD_DIM = 64
ROPE_THETA = 10000.0
LOG2E = 1.4426950408889634
MLA_SCALE = (B_NOPE + B_ROPE) ** -0.5 * LOG2E
FOX_SCALE = D_HEAD_DIM ** -0.5 * LOG2E
LN_EPS = 1e-5
RMS_EPS = 1e-6
NEG = -1e30
LANE = 128
SUBLANE = 8
VMEM_LIMIT = 56 * 1024 * 1024

_NT = (((1,), (1,)), ((), ()))


def _cp(*sem):
    return pltpu.CompilerParams(dimension_semantics=sem, vmem_limit_bytes=VMEM_LIMIT)


def _ln(y, g, b):
    mu = jnp.mean(y, -1, keepdims=True)
    d = y - mu
    var = jnp.mean(d * d, -1, keepdims=True)
    return d * lax.rsqrt(var + LN_EPS) * g + b


def _rms(y, g):
    return y * lax.rsqrt(jnp.mean(y * y, -1, keepdims=True) + RMS_EPS) * g


def _split3(x):
    hi = x.astype(BF16)
    r = x - hi.astype(F32)
    mid = r.astype(BF16)
    lo = (r - mid.astype(F32)).astype(BF16)
    return hi, mid, lo


def _dot(a, b):
    return jnp.dot(a, b, preferred_element_type=F32)


def _dot_nt(a, b):
    return lax.dot_general(a, b, _NT, preferred_element_type=F32)


def _mod_spec(r, d):
    if r == 1:
        return pl.BlockSpec((1, 6, 1, d), lambda b, i, *_: (b, 0, 0, 0))
    return pl.BlockSpec((1, 6, r, d), lambda b, i, *_: (b, 0, i, 0))


def _row_spec(tm, c):
    return pl.BlockSpec((1, tm, c), lambda b, i, *_: (b, i, 0))


def _time_minor_spec(c, tm):
    return pl.BlockSpec((1, c, tm), lambda b, i, *_: (b, 0, i))


def _full_spec(shape):
    return pl.BlockSpec(shape, lambda *_: (0,) * len(shape))


def _ada_kernel(c_ref, w_ref, b_ref, o_ref):
    c = c_ref[...]
    s = (c * jax.nn.sigmoid(c)).astype(BF16)
    o_ref[0, 0] = _dot(s, w_ref[0].astype(BF16)) + b_ref[0]


def _ada(c_all, w_ada, b_ada):
    depth, d, d6 = w_ada.shape
    n = c_all.shape[0]
    return pl.pallas_call(
        _ada_kernel, grid=(depth, d6 // d),
        in_specs=[pl.BlockSpec((n, d), lambda l, j: (0, 0)),
                  pl.BlockSpec((1, d, d), lambda l, j: (l, 0, j)),
                  pl.BlockSpec((1, 1, d), lambda l, j: (l, 0, j))],
        out_specs=pl.BlockSpec((1, 1, n, d), lambda l, j: (l, j, 0, 0)),
        out_shape=jax.ShapeDtypeStruct((depth, d6 // d, n, d), F32),
        compiler_params=_cp("arbitrary", "arbitrary"), name="ada")(c_all, w_ada, b_ada.reshape(depth, 1, d6))


E_XA, E_GB, E_GC, E_QA, E_KV, E_KR, E_KRR, E_END = 0, 512, 1024, 1536, 1920, 2176, 2208, 2240
E_COLS = 2304
Q_NOPE, Q_ROPE, Q_ROT, Q_END = 0, 512, 768, 1024
MLA_RANK = E_KR - E_KV
MLA_KC = 384


def _even_in_kernel(x_ref, mod_ref, w_ref, qn_ref, kvn_ref, wq_ref, ck_ref, sk_ref, cq_ref, sq_ref,
                    u_ref, gb_ref, ckv_ref, kr_ref, kcat_ref, qnope_ref, qrope_ref):
    m = mod_ref[0]
    h = x_ref[0] * (1.0 + m[1]) + m[0]
    z = _dot(h.astype(BF16), w_ref[...])
    u_ref[0] = z[:, E_GC:E_QA] * z[:, E_XA:E_GB]
    gb_ref[0] = z[:, E_GB:E_GC]
    ckv = _rms(z[:, E_KV:E_KR], kvn_ref[...])
    ckv_ref[0] = ckv
    kr = z[:, E_KR:E_KRR] * ck_ref[...] + z[:, E_KRR:E_END] * sk_ref[...]
    kr_ref[0] = kr
    kcat_ref[0, :, 0:MLA_RANK] = ckv.astype(BF16)
    kcat_ref[0, :, MLA_RANK:MLA_RANK + B_ROPE] = kr.astype(BF16)
    kcat_ref[0, :, MLA_RANK + B_ROPE:] = jnp.zeros((kr.shape[0], MLA_KC - MLA_RANK - B_ROPE), BF16)
    qn = _rms(z[:, E_QA:E_KV], qn_ref[...])
    q = _dot(qn.astype(BF16), wq_ref[...])
    qnope_ref[0] = q[:, Q_NOPE:Q_ROPE].astype(BF16)
    qr = q[:, Q_ROPE:Q_ROT] * cq_ref[...] + q[:, Q_ROT:Q_END] * sq_ref[...]
    qrope_ref[0] = (qr * MLA_SCALE).astype(BF16)


def _even_in(x, mod, w_in, qn, kvn, wq, ck, sk, cq, sq, tm):
    nb, t, d = x.shape
    r = mod.shape[2]
    aw = E_GB - E_XA
    outs = [(aw, F32), (aw, F32), (MLA_RANK, F32), (B_ROPE, F32), (MLA_KC, BF16),
            (B_HEADS * B_NOPE, BF16), (B_HEADS * B_ROPE, BF16)]
    tab = lambda c: pl.BlockSpec((tm, c), lambda b, i: (i, 0))
    return pl.pallas_call(
        _even_in_kernel, grid=(nb, t // tm),
        in_specs=[_row_spec(tm, d), _mod_spec(r, d), _full_spec(w_in.shape), _full_spec(qn.shape),
                  _full_spec(kvn.shape), _full_spec(wq.shape), tab(B_ROPE), tab(B_ROPE),
                  tab(B_HEADS * B_ROPE), tab(B_HEADS * B_ROPE)],
        out_specs=[_row_spec(tm, c) for c, _ in outs],
        out_shape=[jax.ShapeDtypeStruct((nb, t, c), dt) for c, dt in outs],
        compiler_params=_cp("arbitrary", "arbitrary"), name="even_in")(x, mod, w_in, qn, kvn, wq, ck, sk, cq, sq)


def _causal_pairs(t, tq, tk):
    qi, kj, last = [], [], []
    for i in range(t // tq):
        n = ((i + 1) * tq - 1) // tk + 1
        for j in range(n):
            qi.append(i)
            kj.append(j)
            last.append(int(j == n - 1))
    return (np.asarray(qi, np.int32), np.asarray(kj, np.int32), np.asarray(last, np.int32))


ATTN_LOOKAHEAD = 3


def _mla_attn_kernel(qi_ref, kj_ref, last_ref, qn_ref, qr_ref, kc_ref, wkn_ref, wkv_ref, yb_ref,
                     qc_s, m_s, l_s, acc_s, *, tq, tk):
    step = pl.program_id(1)
    qi = qi_ref[step]
    kj = kj_ref[step]

    @pl.when(kj == 0)
    def _init():
        qn = qn_ref[0]
        qr = qr_ref[0]
        for h in range(B_HEADS):
            ql = _dot(qn[:, h * B_NOPE:(h + 1) * B_NOPE], wkn_ref[h]) * MLA_SCALE
            qc_s[h * tq:(h + 1) * tq, 0:MLA_RANK] = ql.astype(BF16)
            qc_s[h * tq:(h + 1) * tq, MLA_RANK:MLA_RANK + B_ROPE] = qr[:, h * B_ROPE:(h + 1) * B_ROPE]
            qc_s[h * tq:(h + 1) * tq, MLA_RANK + B_ROPE:] = jnp.zeros((tq, MLA_KC - MLA_RANK - B_ROPE), BF16)
        m_s[...] = jnp.full_like(m_s, NEG)
        l_s[...] = jnp.zeros_like(l_s)
        acc_s[...] = jnp.zeros_like(acc_s)

    def update(masked, nk):
        kc = kc_ref[0, 0:nk, :]
        v = kc[:, :MLA_RANK]

        def scores(h):
            return _dot_nt(qc_s[h * tq:(h + 1) * tq, :], kc)

        pending = [scores(h) for h in range(ATTN_LOOKAHEAD)]
        for h in range(B_HEADS):
            rows = slice(h * tq, (h + 1) * tq)
            s = pending.pop(0)
            if h + ATTN_LOOKAHEAD < B_HEADS:
                pending.append(scores(h + ATTN_LOOKAHEAD))
            if masked:
                ri = lax.broadcasted_iota(jnp.int32, s.shape, 0)
                ci = lax.broadcasted_iota(jnp.int32, s.shape, 1)
                s = jnp.where(ri - ci >= kj * tk - qi * tq, s, NEG)
            m_prev = m_s[rows, :]
            m_new = jnp.maximum(m_prev, jnp.max(s, -1, keepdims=True))
            alpha = jnp.exp2(m_prev - m_new)
            p = jnp.exp2(s - m_new)
            l_s[rows, :] = alpha * l_s[rows, :] + jnp.sum(p, -1, keepdims=True)
            acc_s[rows, :] = alpha * acc_s[rows, :] + _dot(p.astype(BF16), v)
            m_s[rows, :] = m_new

    visible = (qi + 1) * tq - kj * tk

    @pl.when(visible > tk)
    def _plain():
        update(False, tk)

    for nk in range(tq, tk + tq, tq):
        @pl.when(visible == nk)
        def _masked(nk=nk):
            update(True, nk)

    @pl.when(last_ref[step] == 1)
    def _fin():
        o = acc_s[...] / l_s[...]
        ys = [_dot(o[h * tq:(h + 1) * tq].astype(BF16), wkv_ref[h]) for h in range(B_HEADS)]
        yb_ref[0] = jnp.concatenate(ys, axis=1).astype(yb_ref.dtype)


def _mla_attn(qnope, qrope, kcat, wkn, wkv, tq, tk):
    nb, t, _ = qnope.shape
    qi, kj, last = _causal_pairs(t, tq, tk)
    grid_spec = pltpu.PrefetchScalarGridSpec(
        num_scalar_prefetch=3, grid=(nb, len(qi)),
        in_specs=[pl.BlockSpec((1, tq, qnope.shape[-1]), lambda b, s, qi, kj, la: (b, qi[s], 0)),
                  pl.BlockSpec((1, tq, qrope.shape[-1]), lambda b, s, qi, kj, la: (b, qi[s], 0)),
                  pl.BlockSpec((1, tk, MLA_KC), lambda b, s, qi, kj, la: (b, kj[s], 0)),
                  _full_spec(wkn.shape), _full_spec(wkv.shape)],
        out_specs=pl.BlockSpec((1, tq, B_HEADS * B_V), lambda b, s, qi, kj, la: (b, qi[s], 0)),
        scratch_shapes=[pltpu.VMEM((B_HEADS * tq, MLA_KC), BF16),
                        pltpu.VMEM((B_HEADS * tq, 1), F32), pltpu.VMEM((B_HEADS * tq, 1), F32),
                        pltpu.VMEM((B_HEADS * tq, MLA_RANK), F32)])
    return pl.pallas_call(
        functools.partial(_mla_attn_kernel, tq=tq, tk=tk), grid_spec=grid_spec,
        out_shape=jax.ShapeDtypeStruct((nb, t, B_HEADS * B_V), BF16),
        compiler_params=_cp("arbitrary", "arbitrary"), name="mla_attn")(
            jnp.asarray(qi), jnp.asarray(kj), jnp.asarray(last), qnope, qrope, kcat, wkn, wkv)


CONV_ROWS = 32


def _conv_prompt(u, halo_ref, ext_s, w_ref, i, k):
    tm = u.shape[0]
    hr = halo_ref.shape[1]
    ext_s[0, 0:hr] = jnp.where(i > 0, halo_ref[0], 0.0)
    ext_s[0, hr:hr + tm] = u
    n = hr + tm - SUBLANE
    for p in range(1, SUBLANE):
        ext_s[p, 0:n] = ext_s[0, p:p + n]
    off = hr - (k - 1)
    blocks = []
    for r in range(0, tm, CONV_ROWS):
        acc = w_ref[k - 1:k] * ext_s[0, hr + r:hr + r + CONV_ROWS]
        for j in range(k - 1):
            p = (off + j) % SUBLANE
            base = off + j + r - p
            acc = acc + w_ref[j:j + 1] * ext_s[p, base:base + CONV_ROWS]
        blocks.append(acc)
    return jnp.concatenate(blocks, axis=0)


def _conv_sample(u, hist_ref, w_ref, k):
    acc = w_ref[k - 1:k] * u
    for j in range(k - 1):
        acc = acc + w_ref[j:j + 1] * hist_ref[j]
    return acc


def _halo_spec(tm, hr, c):
    return pl.BlockSpec((1, hr, c), lambda b, i: (b, jnp.maximum(i * (tm // hr) - 1, 0), 0))


def _even_out_kernel(*refs, alpha, sample):
    if sample:
        (u_ref, hist_ref, gb_ref, yb_ref, x_ref, mod_ref, cw_ref, wa_ref, wb_ref, g_ref, b_ref, o_ref) = refs
        conv = _conv_sample(u_ref[0], hist_ref, cw_ref, A_CONV)
    else:
        (u_ref, halo_ref, gb_ref, yb_ref, x_ref, mod_ref, cw_ref, wa_ref, wb_ref, g_ref, b_ref, o_ref, ext_s) = refs
        conv = _conv_prompt(u_ref[0], halo_ref, ext_s, cw_ref, pl.program_id(1), A_CONV)
    ya = gb_ref[0] * conv
    mix = _dot(ya.astype(BF16), wa_ref[...]) + _dot(yb_ref[0], wb_ref[...])
    m = mod_ref[0]
    o_ref[0] = _ln(alpha * x_ref[0] + (1.0 + m[2]) * mix, g_ref[...], b_ref[...])


def _mixer_out(kernel, u, side, others, other_specs, x, mod, tail, tm, alpha, sample, k):
    nb, t, d = x.shape
    c = u.shape[-1]
    r = mod.shape[2]
    hr = 8 if k - 1 <= 8 else 32
    if sample:
        side_arg, side_spec, scratch = side, _full_spec(side.shape), []
    else:
        side_arg, side_spec, scratch = u, _halo_spec(tm, hr, c), [pltpu.VMEM((SUBLANE, tm + hr, c), F32)]
    return pl.pallas_call(
        functools.partial(kernel, alpha=alpha, sample=sample), grid=(nb, t // tm),
        in_specs=[_row_spec(tm, c), side_spec] + list(other_specs)
        + [_row_spec(tm, d), _mod_spec(r, d)] + [_full_spec(w.shape) for w in tail],
        out_specs=_row_spec(tm, d), out_shape=jax.ShapeDtypeStruct((nb, t, d), F32),
        scratch_shapes=scratch, compiler_params=_cp("arbitrary", "arbitrary"),
        name=kernel.__name__.strip("_"))(u, side_arg, *others, x, mod, *tail)


def _ffn_kernel(x_ref, mod_ref, w1_ref, w2_ref, g_ref, b_ref, o_ref, h_s, acc_s, *, alpha):
    f = pl.program_id(2)

    @pl.when(f == 0)
    def _first():
        m = mod_ref[0]
        h_s[...] = (x_ref[0] * (1.0 + m[4]) + m[3]).astype(BF16)
        acc_s[...] = jnp.zeros_like(acc_s)

    a = jnp.square(jnp.maximum(_dot(h_s[...], w1_ref[...]), 0.0))
    acc_s[...] += _dot(a.astype(BF16), w2_ref[...])

    @pl.when(f == pl.num_programs(2) - 1)
    def _last():
        m = mod_ref[0]
        o_ref[0] = _ln(alpha * x_ref[0] + (1.0 + m[5]) * acc_s[...], g_ref[...], b_ref[...])


def _ffn(x, mod, w1, w2, g, b, tm, tf, alpha):
    nb, t, d = x.shape
    r = mod.shape[2]
    ff = w1.shape[1]
    return pl.pallas_call(
        functools.partial(_ffn_kernel, alpha=alpha), grid=(nb, t // tm, ff // tf),
        in_specs=[_row_spec(tm, d), _mod_spec(r, d),
                  pl.BlockSpec((d, tf), lambda b, i, f: (0, f)), pl.BlockSpec((tf, d), lambda b, i, f: (f, 0)),
                  _full_spec(g.shape), _full_spec(b.shape)],
        out_specs=_row_spec(tm, d), out_shape=jax.ShapeDtypeStruct((nb, t, d), F32),
        scratch_shapes=[pltpu.VMEM((tm, d), BF16), pltpu.VMEM((tm, d), F32)],
        compiler_params=_cp("arbitrary", "arbitrary", "arbitrary"), name="ffn")(x, mod, w1, w2, g, b)


O_GA, O_GB, O_Q, O_K, O_V, O_F, O_END = 0, 512, 1024, 1536, 2048, 2560, 2568
O_COLS = 2688


def _log_sigmoid(x):
    return jnp.minimum(x, 0.0) - jnp.log(1.0 + jnp.exp(-jnp.abs(x)))


FOX_KC = 128
FOX_DW = O_K - O_Q


def _fox_key_placement():
    p = np.zeros((FOX_DW + 3 * LANE, D_HEADS * FOX_KC), np.float32)
    for h in range(D_HEADS):
        for c in range(D_HEAD_DIM):
            p[h * D_HEAD_DIM + c, h * FOX_KC + c] = 1.0
        for piece in range(3):
            p[FOX_DW + piece * LANE + h, h * FOX_KC + D_HEAD_DIM + piece] = 1.0
    return jnp.asarray(p, BF16)


def _odd_in_kernel(*refs, prompt):
    if prompt:
        (x_ref, mod_ref, w_ref, bf_ref, place_ref, u_ref, kt_ref, vt_ref, lf_ref, ka_ref, qt_ref, vtb_ref,
         carry_s) = refs
    else:
        (x_ref, mod_ref, w_ref, bf_ref, u_ref, q_ref, k_ref, v_ref, lf_ref) = refs
    m = mod_ref[0]
    h = x_ref[0] * (1.0 + m[1]) + m[0]
    z = _dot(h.astype(BF16), w_ref[...])
    tm = z.shape[0]
    u_ref[0] = z[:, O_GA:O_GB] * jax.nn.sigmoid(z[:, O_GB:O_Q])
    q = z[:, O_Q:O_K] * FOX_SCALE
    k = z[:, O_K:O_V]
    v = z[:, O_V:O_F]
    lane = lax.broadcasted_iota(jnp.int32, (tm, O_COLS - O_F), 1)
    lf = jnp.where(lane < D_HEADS, _log_sigmoid(z[:, O_F:O_COLS] + bf_ref[...]), 0.0)
    lf_ref[0] = lf[:, :D_HEADS]
    if not prompt:
        q_ref[0] = q.astype(BF16)
        k_ref[0] = k
        v_ref[0] = v
        return

    kt_ref[0] = k.T
    vt = v.T
    vt_ref[0] = vt
    vtb_ref[0] = vt.astype(BF16)
    qt_ref[0] = q.T.astype(BF16)

    @pl.when(pl.program_id(1) == 0)
    def _reset():
        carry_s[...] = jnp.zeros_like(carry_s)

    tri = (lax.broadcasted_iota(jnp.int32, (tm, tm), 0) >= lax.broadcasted_iota(jnp.int32, (tm, tm), 1))
    tri = jnp.where(tri, 1.0, 0.0).astype(BF16)
    cum = sum(_dot(tri, piece) for piece in _split3(lf)) + carry_s[...]
    carry_s[...] = cum[tm - 1:tm]
    cat = jnp.concatenate((k.astype(BF16),) + _split3(cum * LOG2E), axis=1)
    ka_ref[0] = _dot(cat, place_ref[...]).astype(BF16)


def _odd_in(x, mod, w_in, bf, tm, prompt):
    nb, t, d = x.shape
    r = mod.shape[2]
    cw = O_GB - O_GA
    ins = [x, mod, w_in, bf]
    in_specs = [_row_spec(tm, d), _mod_spec(r, d), _full_spec(w_in.shape), _full_spec(bf.shape)]
    if prompt:
        place = _fox_key_placement()
        ins.append(place)
        in_specs.append(_full_spec(place.shape))
        rows = [(cw, F32)]
        cols = [(FOX_DW, F32), (FOX_DW, F32)]
        rows2 = [(D_HEADS, F32), (D_HEADS * FOX_KC, BF16)]
        cols2 = [(FOX_DW, BF16), (FOX_DW, BF16)]
        out_specs = ([_row_spec(tm, c) for c, _ in rows] + [_time_minor_spec(c, tm) for c, _ in cols]
                     + [_row_spec(tm, c) for c, _ in rows2] + [_time_minor_spec(c, tm) for c, _ in cols2])
        out_shape = ([jax.ShapeDtypeStruct((nb, t, c), dt) for c, dt in rows]
                     + [jax.ShapeDtypeStruct((nb, c, t), dt) for c, dt in cols]
                     + [jax.ShapeDtypeStruct((nb, t, c), dt) for c, dt in rows2]
                     + [jax.ShapeDtypeStruct((nb, c, t), dt) for c, dt in cols2])
        scratch = [pltpu.VMEM((1, O_COLS - O_F), F32)]
    else:
        outs = [(cw, F32), (FOX_DW, BF16), (FOX_DW, F32), (FOX_DW, F32), (D_HEADS, F32)]
        out_specs = [_row_spec(tm, c) for c, _ in outs]
        out_shape = [jax.ShapeDtypeStruct((nb, t, c), dt) for c, dt in outs]
        scratch = []
    return pl.pallas_call(
        functools.partial(_odd_in_kernel, prompt=prompt), grid=(nb, t // tm),
        in_specs=in_specs, out_specs=out_specs, out_shape=out_shape,
        scratch_shapes=scratch, compiler_params=_cp("arbitrary", "arbitrary"), name="odd_in")(*ins)


def _fox_attn_kernel(qi_ref, kj_ref, last_ref, ka_ref, qt_ref, vt_ref, o_ref, qa_s, m_s, l_s, acc_s, *, tq, tk):
    step = pl.program_id(1)
    qi = qi_ref[step]
    kj = kj_ref[step]
    hd = D_HEAD_DIM

    @pl.when(kj == 0)
    def _init():
        minus_ones = jnp.where(lax.broadcasted_iota(jnp.int32, (FOX_KC - hd, tq), 0) < 3, -1.0, 0.0).astype(BF16)
        for h in range(D_HEADS):
            qa_s[h, 0:hd, :] = qt_ref[0, h * hd:(h + 1) * hd, :]
            qa_s[h, hd:FOX_KC, :] = minus_ones
        m_s[...] = jnp.full_like(m_s, NEG)
        l_s[...] = jnp.zeros_like(l_s)
        acc_s[...] = jnp.zeros_like(acc_s)

    def update(masked):
        def scores(h):
            return _dot(ka_ref[0, :, h * FOX_KC:(h + 1) * FOX_KC], qa_s[h])

        pending = [scores(h) for h in range(ATTN_LOOKAHEAD)]
        for h in range(D_HEADS):
            s = pending.pop(0)
            if h + ATTN_LOOKAHEAD < D_HEADS:
                pending.append(scores(h + ATTN_LOOKAHEAD))
            if masked:
                ki = lax.broadcasted_iota(jnp.int32, s.shape, 0)
                qq = lax.broadcasted_iota(jnp.int32, s.shape, 1)
                s = jnp.where(qq - ki >= kj * tk - qi * tq, s, NEG)
            m_prev = m_s[h]
            m_new = jnp.maximum(m_prev, jnp.max(s, 0, keepdims=True))
            alpha = jnp.exp2(m_prev - m_new)
            p = jnp.exp2(s - m_new)
            l_s[h] = alpha * l_s[h] + jnp.sum(p, 0, keepdims=True)
            acc_s[h] = alpha * acc_s[h] + _dot(vt_ref[0, h * hd:(h + 1) * hd, :], p.astype(BF16))
            m_s[h] = m_new

    needs_mask = (kj + 1) * tk - 1 > qi * tq

    @pl.when(needs_mask)
    def _masked():
        update(True)

    @pl.when(jnp.logical_not(needs_mask))
    def _plain():
        update(False)

    @pl.when(last_ref[step] == 1)
    def _fin():
        for h in range(D_HEADS):
            o_ref[0, h * hd:(h + 1) * hd, :] = (acc_s[h] / l_s[h]).astype(o_ref.dtype)


def _fox_attn(ka, q_t, v_t, tq, tk):
    nb, t, _ = ka.shape
    dw = v_t.shape[1]
    qi, kj, last = _causal_pairs(t, tq, tk)
    qmap = lambda b, s, qi, kj, la: (b, 0, qi[s])
    grid_spec = pltpu.PrefetchScalarGridSpec(
        num_scalar_prefetch=3, grid=(nb, len(qi)),
        in_specs=[pl.BlockSpec((1, tk, ka.shape[-1]), lambda b, s, qi, kj, la: (b, kj[s], 0)),
                  pl.BlockSpec((1, dw, tq), qmap),
                  pl.BlockSpec((1, dw, tk), lambda b, s, qi, kj, la: (b, 0, kj[s]))],
        out_specs=pl.BlockSpec((1, dw, tq), qmap),
        scratch_shapes=[pltpu.VMEM((D_HEADS, FOX_KC, tq), BF16), pltpu.VMEM((D_HEADS, 1, tq), F32),
                        pltpu.VMEM((D_HEADS, 1, tq), F32), pltpu.VMEM((D_HEADS, D_HEAD_DIM, tq), F32)])
    return pl.pallas_call(
        functools.partial(_fox_attn_kernel, tq=tq, tk=tk), grid_spec=grid_spec,
        out_shape=jax.ShapeDtypeStruct((nb, dw, t), BF16),
        compiler_params=_cp("arbitrary", "arbitrary"), name="fox_attn")(
            jnp.asarray(qi), jnp.asarray(kj), jnp.asarray(last), ka, q_t, v_t)


def _odd_out_kernel(*refs, alpha, sample):
    if sample:
        (u_ref, hist_ref, o_in_ref, x_ref, mod_ref, cw_ref, cb_ref, cg_ref, cbb_ref, wa_ref, wb_ref, g_ref, b_ref,
         o_ref) = refs
        conv = _conv_sample(u_ref[0], hist_ref, cw_ref, C_CONV)
    else:
        (u_ref, halo_ref, o_in_ref, x_ref, mod_ref, cw_ref, cb_ref, cg_ref, cbb_ref, wa_ref, wb_ref, g_ref, b_ref,
         o_ref, ext_s) = refs
        conv = _conv_prompt(u_ref[0], halo_ref, ext_s, cw_ref, pl.program_id(1), C_CONV)
    y = _ln(conv + cb_ref[...], cg_ref[...], cbb_ref[...])
    yc = y * jax.nn.sigmoid(y)
    attn = lax.dot_general(o_in_ref[0], wb_ref[...], (((0,), (0,)), ((), ())), preferred_element_type=F32)
    mix = _dot(yc.astype(BF16), wa_ref[...]) + attn
    m = mod_ref[0]
    o_ref[0] = _ln(alpha * x_ref[0] + (1.0 + m[2]) * mix, g_ref[...], b_ref[...])


def _qlat_kernel(qn_ref, wkn_ref, o_ref):
    qn = qn_ref[...]
    for h in range(B_HEADS):
        o_ref[h] = (_dot(qn[:, h * B_NOPE:(h + 1) * B_NOPE], wkn_ref[h]) * MLA_SCALE).astype(BF16)


def _qlat(qnope, wkn):
    n = qnope.shape[0]
    return pl.pallas_call(_qlat_kernel, out_shape=jax.ShapeDtypeStruct((B_HEADS, n, wkn.shape[-1]), BF16),
                          name="qlat")(qnope, wkn)


def _yb_kernel(o_ref, wkv_ref, y_ref):
    ys = [_dot(o_ref[h].astype(BF16), wkv_ref[h]) for h in range(B_HEADS)]
    y_ref[...] = jnp.concatenate(ys, axis=1).astype(y_ref.dtype)


def _yb(o_lat_t, wkv):
    n = o_lat_t.shape[1]
    return pl.pallas_call(_yb_kernel, out_shape=jax.ShapeDtypeStruct((n, B_HEADS * B_V), BF16),
                          name="yb")(o_lat_t, wkv)


def _online_update(s, m_s, l_s, acc_s, values, values_transposed):
    m_prev = m_s[...]
    m_new = jnp.maximum(m_prev, jnp.max(s, -1, keepdims=True))
    alpha = jnp.exp2(m_prev - m_new)
    p = jnp.exp2(s - m_new)
    l_s[...] = alpha * l_s[...] + jnp.sum(p, -1, keepdims=True)
    pv = _dot_nt(p.astype(BF16), values) if values_transposed else _dot(p.astype(BF16), values)
    acc_s[...] = alpha * acc_s[...] + pv
    m_s[...] = m_new


def _mla_dec_kernel(pt_ref, ql_ref, qr_ref, cn_ref, kn_ref, *rest, npg, page):
    lat_refs, kr_refs = rest[:npg], rest[npg:2 * npg]
    o_ref, lat_s, kr_s, m_s, l_s, acc_s = rest[2 * npg:]
    c = pl.program_id(1)

    @pl.when(c == 0)
    def _init():
        m_s[...] = jnp.full_like(m_s, NEG)
        l_s[...] = jnp.zeros_like(l_s)
        acc_s[...] = jnp.zeros_like(acc_s)

    for i in range(npg):
        lat_s[i * page:(i + 1) * page, :] = lat_refs[i][0, 0].astype(BF16)
        kr_s[:, i * page:(i + 1) * page] = kr_refs[i][0, 0].astype(BF16)
    ql = ql_ref[0]
    qr = qr_ref[0]
    s = _dot_nt(ql, lat_s[...]) + _dot(qr, kr_s[...])
    _online_update(s, m_s, l_s, acc_s, lat_s[...], False)

    @pl.when(c == pl.num_programs(1) - 1)
    def _fin():
        cn = cn_ref[0].astype(BF16).astype(F32)
        kn = kn_ref[0].astype(BF16).astype(F32)
        s_new = (jnp.sum(ql.astype(F32) * cn, -1, keepdims=True) + jnp.sum(qr.astype(F32) * kn, -1, keepdims=True))
        m_prev = m_s[...]
        m_new = jnp.maximum(m_prev, s_new)
        alpha = jnp.exp2(m_prev - m_new)
        p_new = jnp.exp2(s_new - m_new)
        o_ref[0] = (alpha * acc_s[...] + p_new * cn) / (alpha * l_s[...] + p_new)


def _page_map(layer, n_pages, npg, i):
    return lambda b, c, pt: (layer, pt[b * n_pages + c * npg + i], 0, 0)


def _mla_dec(pt_flat, ql, qr, cn, kn, lat_pool, krt_pool, layer, n_pages, npg):
    ns = ql.shape[0]
    page, rank = lat_pool.shape[2], lat_pool.shape[3]
    seq = lambda shape: pl.BlockSpec((1,) + shape, lambda b, c, pt: (b, 0, 0))
    grid_spec = pltpu.PrefetchScalarGridSpec(
        num_scalar_prefetch=1, grid=(ns, n_pages // npg),
        in_specs=[seq((B_HEADS, rank)), seq((B_HEADS, B_ROPE)), seq((1, rank)), seq((1, B_ROPE))]
        + [pl.BlockSpec((1, 1, page, rank), _page_map(layer, n_pages, npg, i)) for i in range(npg)]
        + [pl.BlockSpec((1, 1, B_ROPE, page), _page_map(layer, n_pages, npg, i)) for i in range(npg)],
        out_specs=seq((B_HEADS, rank)),
        scratch_shapes=[pltpu.VMEM((npg * page, rank), BF16), pltpu.VMEM((B_ROPE, npg * page), BF16),
                        pltpu.VMEM((B_HEADS, 1), F32), pltpu.VMEM((B_HEADS, 1), F32),
                        pltpu.VMEM((B_HEADS, rank), F32)])
    return pl.pallas_call(
        functools.partial(_mla_dec_kernel, npg=npg, page=page), grid_spec=grid_spec,
        out_shape=jax.ShapeDtypeStruct((ns, B_HEADS, rank), F32),
        compiler_params=_cp("arbitrary", "arbitrary"), name="mla_dec")(
            pt_flat, ql, qr, cn, kn, *([lat_pool] * npg), *([krt_pool] * npg))


def _fox_dec_kernel(pt_ref, q_ref, kn_ref, vn_ref, lfn_ref, *rest, npg, page):
    k_refs, v_refs, lf_refs = rest[:npg], rest[npg:2 * npg], rest[2 * npg:3 * npg]
    o_ref, k_s, v_s, m_s, l_s, acc_s, carry_s = rest[3 * npg:]
    c = pl.program_id(1)
    nh = D_HEADS
    width = nh * D_HEAD_DIM

    @pl.when(c == 0)
    def _init():
        m_s[...] = jnp.full_like(m_s, NEG)
        l_s[...] = jnp.zeros_like(l_s)
        acc_s[...] = jnp.zeros_like(acc_s)
        carry_s[...] = jnp.zeros_like(carry_s)

    for i in range(npg):
        k_s[:, i * page:(i + 1) * page] = k_refs[i][0, 0].astype(BF16)
        v_s[:, i * page:(i + 1) * page] = v_refs[i][0, 0].astype(BF16)

    diag = (lax.broadcasted_iota(jnp.int32, (nh, width), 1) // D_HEAD_DIM
            == lax.broadcasted_iota(jnp.int32, (nh, width), 0))
    qbd = jnp.where(diag, q_ref[0].astype(F32), 0.0)

    rows = nh * npg
    y = jnp.concatenate([lf_refs[i][0, 0] for i in range(npg)], axis=0)
    upper = (lax.broadcasted_iota(jnp.int32, (page, page), 0) <= lax.broadcasted_iota(jnp.int32, (page, page), 1))
    upper = jnp.where(upper, 1.0, 0.0).astype(BF16)
    z = sum(_dot(piece, upper) for piece in _split3(y))
    r0 = lax.broadcasted_iota(jnp.int32, (rows, rows), 0)
    r1 = lax.broadcasted_iota(jnp.int32, (rows, rows), 1)
    earlier = jnp.where(((r0 % nh) == (r1 % nh)) & (r1 // nh < r0 // nh), 1.0, 0.0).astype(BF16)
    totals = jnp.broadcast_to(z[:, page - 1:page], (rows, page))
    off = sum(_dot(earlier, piece) for piece in _split3(totals))
    cum = z + off + jnp.concatenate([carry_s[...]] * npg, axis=0)
    carry_s[...] = cum[rows - nh:rows, page - 1:page]
    bias = jnp.concatenate([cum[nh * i:nh * (i + 1), :] for i in range(npg)], axis=1)

    s = _dot(qbd.astype(BF16), k_s[...]) - bias * LOG2E
    _online_update(s, m_s, l_s, acc_s, v_s[...], True)

    @pl.when(c == pl.num_programs(1) - 1)
    def _fin():
        kn = kn_ref[0].astype(BF16).astype(F32)
        vn = vn_ref[0].astype(BF16).astype(F32)
        s_new = jnp.sum(qbd * kn, -1, keepdims=True) - (carry_s[...] + lfn_ref[0]) * LOG2E
        m_prev = m_s[...]
        m_new = jnp.maximum(m_prev, s_new)
        alpha = jnp.exp2(m_prev - m_new)
        p_new = jnp.exp2(s_new - m_new)
        o = (alpha * acc_s[...] + p_new * vn) / (alpha * l_s[...] + p_new)
        o_ref[0] = jnp.sum(jnp.where(diag, o, 0.0), axis=0, keepdims=True)


def _fox_dec(pt_flat, q, kn, vn, lfn, kt_pool, vt_pool, lft_pool, layer, n_pages, npg):
    ns = q.shape[0]
    width, page = kt_pool.shape[2], kt_pool.shape[3]
    seq = lambda shape: pl.BlockSpec((1,) + shape, lambda b, c, pt: (b, 0, 0))
    pages = lambda shape: [pl.BlockSpec((1, 1) + shape, _page_map(layer, n_pages, npg, i)) for i in range(npg)]
    grid_spec = pltpu.PrefetchScalarGridSpec(
        num_scalar_prefetch=1, grid=(ns, n_pages // npg),
        in_specs=[seq((1, width)), seq((1, width)), seq((1, width)), seq((D_HEADS, 1))]
        + pages((width, page)) + pages((width, page)) + pages((D_HEADS, page)),
        out_specs=seq((1, width)),
        scratch_shapes=[pltpu.VMEM((width, npg * page), BF16), pltpu.VMEM((width, npg * page), BF16),
                        pltpu.VMEM((D_HEADS, 1), F32), pltpu.VMEM((D_HEADS, 1), F32),
                        pltpu.VMEM((D_HEADS, width), F32), pltpu.VMEM((D_HEADS, 1), F32)])
    return pl.pallas_call(
        functools.partial(_fox_dec_kernel, npg=npg, page=page), grid_spec=grid_spec,
        out_shape=jax.ShapeDtypeStruct((ns, 1, width), F32),
        compiler_params=_cp("arbitrary", "arbitrary"), name="fox_dec")(
            pt_flat, q, kn, vn, lfn, *([kt_pool] * npg), *([vt_pool] * npg), *([lft_pool] * npg))


def _rope_tables(pos):
    half = B_ROPE // 2
    inv = ROPE_THETA ** (-jnp.arange(half, dtype=F32) / half)
    ang = pos.astype(F32)[:, None] * inv
    cos = jnp.concatenate([jnp.cos(ang)] * 2, axis=-1)
    sin = jnp.concatenate([jnp.sin(ang)] * 2, axis=-1)
    return cos, sin, jnp.tile(cos, (1, B_HEADS)), jnp.tile(sin, (1, B_HEADS))


def _rot_cols(w):
    half = w.shape[-1] // 2
    return jnp.concatenate([-w[..., half:], w[..., :half]], axis=-1)


def _prep_even(w_in, w_q_b, w_kv_b):
    d = w_in.shape[0]
    kr = w_in[:, E_KR:E_KRR]
    w_in_p = jnp.concatenate([w_in, _rot_cols(kr), jnp.zeros((d, E_COLS - E_END), w_in.dtype)], axis=1)
    qr = w_q_b.shape[0]
    wq = w_q_b.reshape(qr, B_HEADS, B_NOPE + B_ROPE)
    rp = wq[:, :, B_NOPE:]
    wq_p = jnp.concatenate([wq[:, :, :B_NOPE].reshape(qr, -1), rp.reshape(qr, -1), _rot_cols(rp).reshape(qr, -1)],
                           axis=1)
    wkn = jnp.transpose(w_kv_b[:, :, :B_NOPE], (1, 2, 0))
    wkv = jnp.transpose(w_kv_b[:, :, B_NOPE:], (1, 0, 2))
    return w_in_p.astype(BF16), wq_p.astype(BF16), wkn.astype(BF16), wkv.astype(BF16)


def kernel(x_prompt, x_sample, cache_mla_latent, cache_mla_krope, state_conv_a, cache_fox_k, cache_fox_v,
           cache_fox_logf, state_conv_c, page_table, c_prompt, c_sample, w_ada, b_ada, ln_g, ln_b,
           w_in_even, w_q_b, w_kv_b, q_norm, kv_norm, conv_a, w_out_even, w_in_odd, b_f, conv_c,
           conv_c_b, cn_g, cn_b, w_out_odd, w_ff1, w_ff2):
    depth = w_ada.shape[0]
    alpha = float((2 * depth) ** 0.25)
    nbp, seq, d = x_prompt.shape
    ns = x_sample.shape[0]
    n_pages = page_table.shape[1]
    page = cache_mla_latent.shape[2]
    past_len = n_pages * page
    n_pool = cache_fox_k.shape[1]
    aw = conv_a.shape[-1]
    cw = conv_c.shape[-1]
    dw = D_HEADS * D_HEAD_DIM

    tm = min(512, seq)
    tq_mla, tk_mla = min(256, seq), min(1024, seq)
    tq_fox, tk_fox = min(512, seq), min(512, seq)
    tm_ffn, tf = min(1024, seq), min(1024, w_ff1.shape[-1])
    npg = min(32, n_pages)

    n_c = nbp + ns
    n_pad = -n_c % 8
    c_all = jnp.concatenate([c_prompt, c_sample, jnp.zeros((n_pad, d), F32)], axis=0)
    mods = _ada(c_all, w_ada, b_ada)
    mod_p = jnp.transpose(mods[:, :, :nbp], (0, 2, 1, 3))[:, :, :, None, :]
    mod_s = mods[:, None, :, nbp:n_c]

    tabs_p = _rope_tables(jnp.arange(seq, dtype=jnp.int32))
    tabs_s = _rope_tables(jnp.full((ns,), past_len, dtype=jnp.int32))
    pt_flat = page_table.reshape(-1)
    keys_minor = lambda a: jnp.transpose(a, (0, 1, 3, 4, 2)).reshape(a.shape[:2] + (-1, a.shape[2]))
    kt_pool = keys_minor(cache_fox_k)
    vt_pool = keys_minor(cache_fox_v)
    lft_pool = jnp.swapaxes(cache_fox_logf, 2, 3)
    krt_pool = jnp.swapaxes(cache_mla_krope, 2, 3)

    xp = x_prompt
    xs = x_sample.reshape(1, ns, d)
    even_p, even_s, odd_p, odd_s = [], [], [], []
    for l in range(depth):
        i = l // 2
        row = lambda a: a.reshape(1, -1)
        g0, b0, g1, b1 = row(ln_g[l, 0]), row(ln_b[l, 0]), row(ln_g[l, 1]), row(ln_b[l, 1])
        if l % 2 == 0:
            w_in_p, wq_p, wkn, wkv = _prep_even(w_in_even[i], w_q_b[i], w_kv_b[i])
            w_out = w_out_even[i].astype(BF16)
            tail = (conv_a[i], w_out[:aw], w_out[aw:], g0, b0)
            qn, kvn = row(q_norm[i]), row(kv_norm[i])
            u, gb, ckv, kr, kcat, qnope, qrope = _even_in(xp, mod_p[l], w_in_p, qn, kvn, wq_p, *tabs_p, tm)
            yb = _mla_attn(qnope, qrope, kcat, wkn, wkv, tq_mla, tk_mla)
            xp = _mixer_out(_even_out_kernel, u, None, (gb, yb), (_row_spec(tm, aw), _row_spec(tm, yb.shape[-1])),
                            xp, mod_p[l], tail, tm, alpha, False, A_CONV)
            even_p.append((ckv, kr, u[:, seq - (A_CONV - 1):]))
            u, gb, ckv, kr, _, qnope, qrope = _even_in(xs, mod_s[l], w_in_p, qn, kvn, wq_p, *tabs_s, ns)
            ql = jnp.transpose(_qlat(qnope[0], wkn), (1, 0, 2))
            qr = qrope[0].reshape(ns, B_HEADS, B_ROPE)
            o_lat = _mla_dec(pt_flat, ql, qr, ckv[0][:, None], kr[0][:, None], cache_mla_latent, krt_pool,
                             i, n_pages, npg)
            yb = _yb(jnp.transpose(o_lat, (1, 0, 2)), wkv)[None]
            hist = jnp.transpose(state_conv_a[i], (1, 0, 2))
            xs = _mixer_out(_even_out_kernel, u, hist, (gb, yb), (_row_spec(ns, aw), _row_spec(ns, yb.shape[-1])),
                            xs, mod_s[l], tail, ns, alpha, True, A_CONV)
            even_s.append((ckv[0][:, None], kr[0][:, None],
                           jnp.concatenate([state_conv_a[i][:, 1:], u[0][:, None]], axis=1)))
        else:
            w_in_p = jnp.concatenate([w_in_odd[i], jnp.zeros((d, O_COLS - O_END), F32)], axis=1).astype(BF16)
            w_out = w_out_odd[i].astype(BF16)
            tail = (conv_c[i], row(conv_c_b[i]), row(cn_g[i]), row(cn_b[i]), w_out[:cw], w_out[cw:], g0, b0)
            bf = jnp.concatenate([b_f[i], jnp.zeros((O_COLS - O_F - D_HEADS,), F32)]).reshape(1, -1)
            u, k_t, v_t, lf, ka, q_tb, v_tb = _odd_in(xp, mod_p[l], w_in_p, bf, tm, True)
            o_t = _fox_attn(ka, q_tb, v_tb, tq_fox, tk_fox)
            xp = _mixer_out(_odd_out_kernel, u, None, (o_t,), (_time_minor_spec(dw, tm),),
                            xp, mod_p[l], tail, tm, alpha, False, C_CONV)
            state = lambda a: jnp.transpose(a.reshape(nbp, D_HEADS, D_HEAD_DIM, seq), (0, 3, 1, 2))
            odd_p.append((state(k_t), state(v_t), lf, u[:, seq - (C_CONV - 1):]))
            u, q, k, v, lf = _odd_in(xs, mod_s[l], w_in_p, bf, ns, False)
            o = _fox_dec(pt_flat, q[0][:, None], k[0][:, None], v[0][:, None], lf[0][:, :, None],
                         kt_pool, vt_pool, lft_pool, i, n_pages, npg)
            o_t = jnp.transpose(o.reshape(ns, dw)).astype(BF16)[None]
            hist = jnp.transpose(state_conv_c[i], (1, 0, 2))
            xs = _mixer_out(_odd_out_kernel, u, hist, (o_t,), (_time_minor_spec(dw, ns),),
                            xs, mod_s[l], tail, ns, alpha, True, C_CONV)
            hd = (ns, 1, D_HEADS, D_HEAD_DIM)
            odd_s.append((k[0].reshape(hd), v[0].reshape(hd), lf[0][:, None],
                          jnp.concatenate([state_conv_c[i][:, 1:], u[0][:, None]], axis=1)))
        w1, w2 = w_ff1[l].astype(BF16), w_ff2[l].astype(BF16)
        xp = _ffn(xp, mod_p[l], w1, w2, g1, b1, tm_ffn, tf, alpha)
        xs = _ffn(xs, mod_s[l], w1, w2, g1, b1, ns, tf, alpha)

    stack = lambda states: [jnp.stack(s) for s in zip(*states)]
    lat_p, kr_p, ca_p = stack(even_p)
    k_p, v_p, lf_p, cc_p = stack(odd_p)
    lat_s, kr_s, ca_s = stack(even_s)
    k_s, v_s, lf_s, cc_s = stack(odd_s)
    return (xp, xs.reshape(ns, 1, d), lat_p, kr_p, ca_p, k_p, v_p, lf_p, cc_p,
            lat_s, kr_s, ca_s, k_s, v_s, lf_s, cc_s)
```
